```python
import jax, jax.numpy as jnp
from jax import lax
import numpy as np

D_MODEL = 1024
BATCH = 16
SEQ = 2048
DEPTH = 4
DEC_BATCH = 32
DEC_SEQ = 2048
PAST_LEN = 128

GRID_W = 64
Q_BLOCK = 128
ROPE_THETA = 10000.0
EPS = 1e-6
N_HEADS_A = 8
N_KV_HEADS_A = 2
HEAD_DIM_A = 64
N_HEADS_B = 8
QK_NOPE_DIM = 64
QK_ROPE_DIM = 32
V_DIM_B = 64
Q_LORA = 256
KV_LORA = 128
WIDTH_A = N_HEADS_A * HEAD_DIM_A
WIDTH_B = N_HEADS_B * V_DIM_B
D_MIX = WIDTH_A + WIDTH_B
KV_WIDTH_A = N_KV_HEADS_A * HEAD_DIM_A
D_IN = WIDTH_A + 2 * KV_WIDTH_A + Q_LORA + KV_LORA + QK_ROPE_DIM
D_FF = 2816
CONV_W = 3
N_MOD = 6

kernel_name = "hybrid_gqa_mla_convffn_encoder"


def rms_norm(x, g):
    xf = x.astype(jnp.float32)
    y = xf * lax.rsqrt(jnp.mean(xf * xf, axis=-1, keepdims=True) + EPS)
    return (y * g.astype(jnp.float32)).astype(x.dtype)


def axial_rope_tables(n_tokens, rot_dim):
    rows = n_tokens // GRID_W
    row = jnp.repeat(jnp.arange(rows, dtype=jnp.float32), GRID_W)
    col = jnp.tile(jnp.arange(GRID_W, dtype=jnp.float32), rows)
    quarter = rot_dim // 4
    inv_freq = ROPE_THETA ** (-jnp.arange(quarter, dtype=jnp.float32) / quarter)
    ang = jnp.stack([row, col], axis=-1)[:, :, None] * inv_freq
    return jnp.cos(ang), jnp.sin(ang)


def apply_axial_rope(x, cos, sin):
    d = x.shape[-1]
    q = d // 4
    s_len = x.shape[1]
    bshape = (1, s_len) + (1,) * (x.ndim - 3) + (2, q)
    c = cos.reshape(bshape)
    s = sin.reshape(bshape)
    xr = x.astype(jnp.float32).reshape(x.shape[:-1] + (2, 2, q))
    x1 = xr[..., 0, :]
    x2 = xr[..., 1, :]
    out = jnp.stack([x1 * c - x2 * s, x2 * c + x1 * s], axis=-2)
    return out.reshape(x.shape).astype(x.dtype)


def block_attention(q, k, v):
    b, s, hk, g, dk = q.shape
    dv = v.shape[-1]
    nb = s // Q_BLOCK
    scale = dk ** -0.5
    kf = k.astype(jnp.float32)
    vf = v.astype(jnp.float32)
    qb = q.astype(jnp.float32).reshape(b, nb, Q_BLOCK, hk, g, dk).transpose(1, 0, 2, 3, 4, 5)

    def one_block(qi):
        sc = jnp.einsum('bqhgd,bkhd->bhgqk', qi, kf) * scale
        p = jax.nn.softmax(sc, axis=-1)
        return jnp.einsum('bhgqk,bkhd->bqhgd', p, vf)

    o = lax.map(one_block, qb)
    return o.transpose(1, 0, 2, 3, 4, 5).reshape(b, s, hk, g, dv).astype(q.dtype)


def depthwise_conv_centred(u, w, bias):
    up = jnp.pad(u, ((0, 0), (1, 1), (0, 0)))
    s = u.shape[1]
    return up[:, 0:s] * w[0] + up[:, 1:s + 1] * w[1] + up[:, 2:s + 2] * w[2] + bias


def encoder_layer(x, mod, cos_a, sin_a, cos_b, sin_b, g_attn, w_in, g_q_a, g_k_a, g_cq, w_uq,
                  g_ckv, w_ukv, g_out_a, g_out_b, w_o, g_ffn, w_up, conv_w, conv_b, w_down):
    b, s, _ = x.shape
    shift_a, scale_a, gate_a, shift_f, scale_f, gate_f = jnp.split(mod[:, None, :], N_MOD, axis=-1)

    h = rms_norm(x, g_attn) * (1.0 + scale_a) + shift_a
    proj = h @ w_in
    idx = np.cumsum([WIDTH_A, KV_WIDTH_A, KV_WIDTH_A, Q_LORA, KV_LORA])
    q_a, k_a, v_a, c_q, c_kv, k_r = jnp.split(proj, [int(i) for i in idx], axis=-1)

    q_a = q_a.reshape(b, s, N_KV_HEADS_A, N_HEADS_A // N_KV_HEADS_A, HEAD_DIM_A)
    k_a = k_a.reshape(b, s, N_KV_HEADS_A, HEAD_DIM_A)
    v_a = v_a.reshape(b, s, N_KV_HEADS_A, HEAD_DIM_A)
    q_a = apply_axial_rope(rms_norm(q_a, g_q_a), cos_a, sin_a)
    k_a = apply_axial_rope(rms_norm(k_a, g_k_a), cos_a, sin_a)
    o_a = block_attention(q_a, k_a, v_a).reshape(b, s, WIDTH_A)

    q_b = (rms_norm(c_q, g_cq) @ w_uq).reshape(b, s, N_HEADS_B, QK_NOPE_DIM + QK_ROPE_DIM)
    q_nope, q_rope = jnp.split(q_b, [QK_NOPE_DIM], axis=-1)
    q_rope = apply_axial_rope(q_rope, cos_b, sin_b)
    kv = (rms_norm(c_kv, g_ckv) @ w_ukv).reshape(b, s, N_HEADS_B, QK_NOPE_DIM + V_DIM_B)
    k_nope, v_b = jnp.split(kv, [QK_NOPE_DIM], axis=-1)
    k_rope = apply_axial_rope(k_r, cos_b, sin_b)
    k_rope = jnp.broadcast_to(k_rope[:, :, None, :], (b, s, N_HEADS_B, QK_ROPE_DIM))
    q_full = jnp.concatenate([q_nope, q_rope], axis=-1)[:, :, :, None, :]
    k_full = jnp.concatenate([k_nope, k_rope], axis=-1)
    o_b = block_attention(q_full, k_full, v_b).reshape(b, s, WIDTH_B)

    o = jnp.concatenate([rms_norm(o_a, g_out_a), rms_norm(o_b, g_out_b)], axis=-1) @ w_o
    x = x + gate_a * o

    h = rms_norm(x, g_ffn) * (1.0 + scale_f) + shift_f
    u = depthwise_conv_centred(h @ w_up, conv_w, conv_b)
    val, gate = jnp.split(u, 2, axis=-1)
    y = (jax.nn.silu(gate) * val) @ w_down
    return x + gate_f * y


def encode(x, c, w_ada, b_ada, g_attn, w_in, g_q_a, g_k_a, g_cq, w_uq, g_ckv, w_ukv,
           g_out_a, g_out_b, w_o, g_ffn, w_up, conv_w, conv_b, w_down, g_final):
    n_tokens = x.shape[1]
    cos_a, sin_a = axial_rope_tables(n_tokens, HEAD_DIM_A)
    cos_b, sin_b = axial_rope_tables(n_tokens, QK_ROPE_DIM)
    c_act = jax.nn.silu(c)
    for l in range(DEPTH):
        mod = c_act @ w_ada[l] + b_ada[l]
        x = encoder_layer(x, mod, cos_a, sin_a, cos_b, sin_b, g_attn[l], w_in[l], g_q_a[l], g_k_a[l],
                          g_cq[l], w_uq[l], g_ckv[l], w_ukv[l], g_out_a[l], g_out_b[l], w_o[l],
                          g_ffn[l], w_up[l], conv_w[l], conv_b[l], w_down[l])
    return rms_norm(x, g_final)


def setup_inputs(seed: int = 0) -> dict:
    key = jax.random.key(seed)
    ks = jax.random.split(key, 24)
    f32 = jnp.float32

    def nrm(k, shape, scale):
        return jax.random.normal(k, shape, f32) * scale

    def gain(k, shape):
        return 1.0 + 0.02 * jax.random.normal(k, shape, f32)

    return {
        "x_prompt": nrm(ks[0], (BATCH, SEQ, D_MODEL), 1.0),
        "x_sample": nrm(ks[1], (DEC_BATCH, DEC_SEQ, D_MODEL), 1.0),
        "c_prompt": nrm(ks[2], (BATCH, D_MODEL), 1.0),
        "c_sample": nrm(ks[3], (DEC_BATCH, D_MODEL), 1.0),
        "w_ada": nrm(ks[4], (DEPTH, D_MODEL, N_MOD * D_MODEL), 0.02),
        "b_ada": nrm(ks[5], (DEPTH, N_MOD * D_MODEL), 0.01),
        "g_attn": gain(ks[6], (DEPTH, D_MODEL)),
        "w_in": nrm(ks[7], (DEPTH, D_MODEL, D_IN), D_MODEL ** -0.5),
        "g_q_a": gain(ks[8], (DEPTH, HEAD_DIM_A)),
        "g_k_a": gain(ks[9], (DEPTH, HEAD_DIM_A)),
        "g_cq": gain(ks[10], (DEPTH, Q_LORA)),
        "w_uq": nrm(ks[11], (DEPTH, Q_LORA, N_HEADS_B * (QK_NOPE_DIM + QK_ROPE_DIM)), Q_LORA ** -0.5),
        "g_ckv": gain(ks[12], (DEPTH, KV_LORA)),
        "w_ukv": nrm(ks[13], (DEPTH, KV_LORA, N_HEADS_B * (QK_NOPE_DIM + V_DIM_B)), KV_LORA ** -0.5),
        "g_out_a": gain(ks[14], (DEPTH, WIDTH_A)),
        "g_out_b": gain(ks[15], (DEPTH, WIDTH_B)),
        "w_o": nrm(ks[16], (DEPTH, D_MIX, D_MODEL), D_MIX ** -0.5),
        "g_ffn": gain(ks[17], (DEPTH, D_MODEL)),
        "w_up": nrm(ks[18], (DEPTH, D_MODEL, 2 * D_FF), D_MODEL ** -0.5),
        "conv_w": nrm(ks[19], (DEPTH, CONV_W, 2 * D_FF), CONV_W ** -0.5),
        "conv_b": nrm(ks[20], (DEPTH, 2 * D_FF), 0.01),
        "w_down": nrm(ks[21], (DEPTH, D_FF, D_MODEL), D_FF ** -0.5),
        "g_final": gain(ks[22], (D_MODEL,)),
    }


def reference(x_prompt, x_sample, c_prompt, c_sample, w_ada, b_ada, g_attn, w_in, g_q_a, g_k_a,
              g_cq, w_uq, g_ckv, w_ukv, g_out_a, g_out_b, w_o, g_ffn, w_up, conv_w, conv_b,
              w_down, g_final):
    y_prompt = encode(x_prompt, c_prompt, w_ada, b_ada, g_attn, w_in, g_q_a, g_k_a, g_cq, w_uq,
                      g_ckv, w_ukv, g_out_a, g_out_b, w_o, g_ffn, w_up, conv_w, conv_b, w_down,
                      g_final)
    y_sample = encode(x_sample, c_sample, w_ada, b_ada, g_attn, w_in, g_q_a, g_k_a, g_cq, w_uq,
                      g_ckv, w_ukv, g_out_a, g_out_b, w_o, g_ffn, w_up, conv_w, conv_b, w_down,
                      g_final)
    return (y_prompt, y_sample)
```

```python
import functools
import math

import jax
import jax.numpy as jnp
from jax import lax
from jax.experimental import pallas as pl
from jax.experimental.pallas import tpu as pltpu

F32 = jnp.float32
BF16 = jnp.bfloat16

D_MODEL = 1024
DEPTH = 4
GRID_W = 64
ROPE_THETA = 10000.0
EPS = 1e-6
N_HEADS_A = 8
N_KV_HEADS_A = 2
HEAD_DIM_A = 64
N_HEADS_B = 8
QK_NOPE_DIM = 64
QK_ROPE_DIM = 32
V_DIM_B = 64
Q_LORA = 256
KV_LORA = 128
WIDTH_A = N_HEADS_A * HEAD_DIM_A
WIDTH_B = N_HEADS_B * V_DIM_B
D_MIX = WIDTH_A + WIDTH_B
KV_WIDTH_A = N_KV_HEADS_A * HEAD_DIM_A
D_IN = WIDTH_A + 2 * KV_WIDTH_A + Q_LORA + KV_LORA + QK_ROPE_DIM
D_FF = 2816
N_MOD = 6

LANES = 128
HEAD_SLOT = 128
D_IN_PAD = 1280
ROPE_SLOT_OFF = 64
FF_CHUNK = 256
N_FF_CHUNKS = D_FF // FF_CHUNK
HALO = 16
LOG2E = math.log2(math.e)
VMEM_LIMIT = 56 * 1024 * 1024


def _rsqrt_mean(ss, n):
    return lax.rsqrt(ss * (1.0 / n) + EPS)


def _mod_kernel(c_ref, w_ref, b_ref, o_ref):
    c = c_ref[...]
    ca = c * (1.0 / (1.0 + jnp.exp(-c)))
    o_ref[0] = jnp.dot(ca.astype(BF16), w_ref[0].astype(BF16), preferred_element_type=F32) + b_ref[0]


def _modulation(c, w_ada, b_ada):
    nb = c.shape[0]
    n_col = N_MOD * D_MODEL
    tn = 1024
    return pl.pallas_call(
        _mod_kernel,
        grid=(DEPTH, n_col // tn),
        in_specs=[
            pl.BlockSpec((nb, D_MODEL), lambda l, j: (0, 0)),
            pl.BlockSpec((1, D_MODEL, tn), lambda l, j: (l, 0, j)),
            pl.BlockSpec((1, 1, tn), lambda l, j: (l, 0, j)),
        ],
        out_specs=pl.BlockSpec((1, nb, tn), lambda l, j: (l, 0, j)),
        out_shape=jax.ShapeDtypeStruct((DEPTH, nb, n_col), F32),
        compiler_params=pltpu.CompilerParams(dimension_semantics=("arbitrary", "arbitrary")),
        name="modulation",
    )(c, w_ada, b_ada.reshape(DEPTH, 1, n_col))


def _rope_t(x, c, s, q):
    sw = jnp.concatenate([x[q:2 * q], x[0:q], x[3 * q:4 * q], x[2 * q:3 * q]], axis=0)
    return x * c + sw * s


def _attn_in_kernel(x_ref, mod_ref, gat_ref, win_ref, gq_ref, gk_ref, gcq_ref, wuqt_ref, gckv_ref,
                    wuk_ref, wuvt_ref, ca_ref, sa_ref, cb_ref, sb_ref,
                    qat_ref, ka_ref, vat_ref, qbt_ref, kb_ref, vbt_ref):
    x = x_ref[0]
    tm = x.shape[0]
    mod = mod_ref[0]
    shift = mod[:, 0:D_MODEL]
    scale = mod[:, D_MODEL:2 * D_MODEL]
    r = _rsqrt_mean(jnp.sum(x * x, axis=-1, keepdims=True), D_MODEL)
    h = (x * r) * gat_ref[0] * (1.0 + scale) + shift
    proj = jnp.dot(h.astype(BF16), win_ref[0], preferred_element_type=F32)

    ca = ca_ref[...]
    sa = sa_ref[...]
    cb = cb_ref[...]
    sb = sb_ref[...]
    zeros_half = jnp.zeros((HEAD_DIM_A, tm), F32)

    o_qa = 0
    q_at = proj[:, o_qa:o_qa + WIDTH_A].T
    qscale_a = (HEAD_DIM_A ** -0.5) * LOG2E
    gq = gq_ref[0]
    for j in range(N_HEADS_A):
        xh = q_at[HEAD_DIM_A * j:HEAD_DIM_A * (j + 1)]
        rh = _rsqrt_mean(jnp.sum(xh * xh, axis=0, keepdims=True), HEAD_DIM_A)
        xr = _rope_t((xh * rh) * gq, ca, sa, HEAD_DIM_A // 4) * qscale_a
        g = j // (N_HEADS_A // N_KV_HEADS_A)
        slot = jnp.concatenate([xr, zeros_half] if g == 0 else [zeros_half, xr], axis=0)
        qat_ref[0, HEAD_SLOT * j:HEAD_SLOT * (j + 1), :] = slot.astype(BF16)

    o_ka = WIDTH_A
    k_at = proj[:, o_ka:o_ka + KV_WIDTH_A].T
    gk = gk_ref[0]
    kparts = []
    for g in range(N_KV_HEADS_A):
        xh = k_at[HEAD_DIM_A * g:HEAD_DIM_A * (g + 1)]
        rh = _rsqrt_mean(jnp.sum(xh * xh, axis=0, keepdims=True), HEAD_DIM_A)
        kparts.append(_rope_t((xh * rh) * gk, ca, sa, HEAD_DIM_A // 4))
    ka_ref[0] = jnp.concatenate(kparts, axis=0).T.astype(BF16)
    o_va = o_ka + KV_WIDTH_A
    vat_ref[0] = proj[:, o_va:o_va + KV_WIDTH_A].T.astype(BF16)

    o_cq = o_va + KV_WIDTH_A
    cq_t = proj[:, o_cq:o_cq + Q_LORA].T
    rq = _rsqrt_mean(jnp.sum(cq_t * cq_t, axis=0, keepdims=True), Q_LORA)
    cqn = (cq_t * rq) * gcq_ref[0]
    qb_t = jnp.dot(wuqt_ref[0], cqn.astype(BF16), preferred_element_type=F32)
    qscale_b = ((QK_NOPE_DIM + QK_ROPE_DIM) ** -0.5) * LOG2E
    zeros_pad = jnp.zeros((HEAD_SLOT - QK_NOPE_DIM - QK_ROPE_DIM, tm), F32)
    for j in range(N_HEADS_B):
        base = HEAD_SLOT * j
        nope = qb_t[base:base + QK_NOPE_DIM]
        rope = _rope_t(qb_t[base + ROPE_SLOT_OFF:base + ROPE_SLOT_OFF + QK_ROPE_DIM], cb, sb, QK_ROPE_DIM // 4)
        slot = jnp.concatenate([nope, rope, zeros_pad], axis=0) * qscale_b
        qbt_ref[0, base:base + HEAD_SLOT, :] = slot.astype(BF16)

    o_ckv = o_cq + Q_LORA
    ckv = proj[:, o_ckv:o_ckv + KV_LORA]
    rkv = _rsqrt_mean(jnp.sum(ckv * ckv, axis=-1, keepdims=True), KV_LORA)
    ckvn = (ckv * rkv) * gckv_ref[0]
    k_nope = jnp.dot(ckvn.astype(BF16), wuk_ref[0], preferred_element_type=F32)
    vbt_ref[0] = jnp.dot(wuvt_ref[0], ckvn.T.astype(BF16), preferred_element_type=F32).astype(BF16)
    o_kr = o_ckv + KV_LORA
    kr_t = proj[:, o_kr:o_kr + LANES].T
    kr_rope = _rope_t(kr_t[ROPE_SLOT_OFF:ROPE_SLOT_OFF + QK_ROPE_DIM], cb, sb, QK_ROPE_DIM // 4)
    kr_tile = jnp.concatenate(
        [jnp.zeros((ROPE_SLOT_OFF, tm), F32), kr_rope, zeros_pad], axis=0).T
    for j in range(N_HEADS_B):
        base = HEAD_SLOT * j
        kb_ref[0, :, base:base + HEAD_SLOT] = (k_nope[:, base:base + HEAD_SLOT] + kr_tile).astype(BF16)


def _attn_in(x, mod, l, w, tabs, tm):
    nb, s, _ = x.shape
    grid = (nb, s // tm)
    lay = lambda *blk: pl.BlockSpec((1,) + blk, lambda b, i: (l,) + (0,) * len(blk))
    tab = lambda rows: pl.BlockSpec((rows, tm), lambda b, i: (0, i))
    out_shapes = (
        jax.ShapeDtypeStruct((nb, N_HEADS_A * HEAD_SLOT, s), BF16),
        jax.ShapeDtypeStruct((nb, s, KV_WIDTH_A), BF16),
        jax.ShapeDtypeStruct((nb, KV_WIDTH_A, s), BF16),
        jax.ShapeDtypeStruct((nb, N_HEADS_B * HEAD_SLOT, s), BF16),
        jax.ShapeDtypeStruct((nb, s, N_HEADS_B * HEAD_SLOT), BF16),
        jax.ShapeDtypeStruct((nb, WIDTH_B, s), BF16),
    )
    fm = lambda rows: pl.BlockSpec((1, rows, tm), lambda b, i: (b, 0, i))
    tk = lambda cols: pl.BlockSpec((1, tm, cols), lambda b, i: (b, i, 0))
    return pl.pallas_call(
        _attn_in_kernel,
        grid=grid,
        in_specs=[
            pl.BlockSpec((1, tm, D_MODEL), lambda b, i: (b, i, 0)),
            pl.BlockSpec((1, 1, N_MOD * D_MODEL), lambda b, i: (b, 0, 0)),
            lay(1, D_MODEL),
            lay(D_MODEL, D_IN_PAD),
            lay(HEAD_DIM_A, tm),
            lay(HEAD_DIM_A, tm),
            lay(Q_LORA, tm),
            lay(N_HEADS_B * HEAD_SLOT, Q_LORA),
            lay(1, KV_LORA),
            lay(KV_LORA, N_HEADS_B * HEAD_SLOT),
            lay(WIDTH_B, KV_LORA),
            tab(HEAD_DIM_A), tab(HEAD_DIM_A), tab(QK_ROPE_DIM), tab(QK_ROPE_DIM),
        ],
        out_specs=(fm(N_HEADS_A * HEAD_SLOT), tk(KV_WIDTH_A), fm(KV_WIDTH_A),
                   fm(N_HEADS_B * HEAD_SLOT), tk(N_HEADS_B * HEAD_SLOT), fm(WIDTH_B)),
        out_shape=out_shapes,
        compiler_params=pltpu.CompilerParams(
            dimension_semantics=("arbitrary", "arbitrary"), vmem_limit_bytes=VMEM_LIMIT),
        name="attn_in",
    )(x, mod, w["g_attn"], w["w_in"], w["gq_t"], w["gk_t"], w["gcq_t"], w["w_uq_t"], w["g_ckv"],
      w["w_uk"], w["w_uv_t"], tabs["ca"], tabs["sa"], tabs["cb"], tabs["sb"])


def _head_attention(k, q_t, v_t):
    s_t = jnp.dot(k, q_t, preferred_element_type=F32)
    m = jnp.max(s_t, axis=0, keepdims=True)
    p = jnp.exp2(s_t - m)
    denom = jnp.sum(p, axis=0, keepdims=True)
    o_t = jnp.dot(v_t, p.astype(BF16), preferred_element_type=F32)
    return o_t * (1.0 / denom)


def _attention_kernel(qat_ref, ka_ref, vat_ref, qbt_ref, kb_ref, vbt_ref, ot_ref):
    ka = ka_ref[0]
    for j in range(N_HEADS_A):
        g = j // (N_HEADS_A // N_KV_HEADS_A)
        q_t = qat_ref[0, HEAD_SLOT * j:HEAD_SLOT * (j + 1), :]
        v_t = vat_ref[0, HEAD_DIM_A * g:HEAD_DIM_A * (g + 1), :]
        ot_ref[0, HEAD_DIM_A * j:HEAD_DIM_A * (j + 1), :] = _head_attention(ka, q_t, v_t)
    for j in range(N_HEADS_B):
        k = kb_ref[0, :, HEAD_SLOT * j:HEAD_SLOT * (j + 1)]
        q_t = qbt_ref[0, HEAD_SLOT * j:HEAD_SLOT * (j + 1), :]
        v_t = vbt_ref[0, V_DIM_B * j:V_DIM_B * (j + 1), :]
        ot_ref[0, WIDTH_A + V_DIM_B * j:WIDTH_A + V_DIM_B * (j + 1), :] = _head_attention(k, q_t, v_t)


def _attention(qat, ka, vat, qbt, kb, vbt, tq):
    nb, s, _ = ka.shape
    qspec = lambda rows: pl.BlockSpec((1, rows, tq), lambda b, i: (b, 0, i))
    full = lambda r, c: pl.BlockSpec((1, r, c), lambda b, i: (b, 0, 0))
    return pl.pallas_call(
        _attention_kernel,
        grid=(nb, s // tq),
        in_specs=[qspec(N_HEADS_A * HEAD_SLOT), full(s, KV_WIDTH_A), full(KV_WIDTH_A, s),
                  qspec(N_HEADS_B * HEAD_SLOT), full(s, N_HEADS_B * HEAD_SLOT), full(WIDTH_B, s)],
        out_specs=pl.BlockSpec((1, D_MIX, tq), lambda b, i: (b, 0, i)),
        out_shape=jax.ShapeDtypeStruct((nb, D_MIX, s), F32),
        compiler_params=pltpu.CompilerParams(
            dimension_semantics=("arbitrary", "arbitrary"), vmem_limit_bytes=VMEM_LIMIT),
        name="attention",
    )(qat, ka, vat, qbt, kb, vbt)


def _attn_out_kernel(ot_ref, x_ref, mod_ref, go_ref, wo_ref, gffn_ref, x1_ref, h2_ref):
    o_t = ot_ref[0]
    oa = o_t[:WIDTH_A]
    ob = o_t[WIDTH_A:]
    ra = _rsqrt_mean(jnp.sum(oa * oa, axis=0, keepdims=True), WIDTH_A)
    rb = _rsqrt_mean(jnp.sum(ob * ob, axis=0, keepdims=True), WIDTH_B)
    on = jnp.concatenate([oa * ra, ob * rb], axis=0).T * go_ref[0]
    res = jnp.dot(on.astype(BF16), wo_ref[0], preferred_element_type=F32)
    mod = mod_ref[0]
    gate_a = mod[:, 2 * D_MODEL:3 * D_MODEL]
    shift_f = mod[:, 3 * D_MODEL:4 * D_MODEL]
    scale_f = mod[:, 4 * D_MODEL:5 * D_MODEL]
    x1 = x_ref[0] + gate_a * res
    x1_ref[0] = x1
    r = _rsqrt_mean(jnp.sum(x1 * x1, axis=-1, keepdims=True), D_MODEL)
    h2_ref[0] = ((x1 * r) * gffn_ref[0] * (1.0 + scale_f) + shift_f).astype(BF16)


def _attn_out(o_t, x, mod, l, w, tm):
    nb, s, _ = x.shape
    lay = lambda *blk: pl.BlockSpec((1,) + blk, lambda b, i: (l,) + (0,) * len(blk))
    tok = pl.BlockSpec((1, tm, D_MODEL), lambda b, i: (b, i, 0))
    return pl.pallas_call(
        _attn_out_kernel,
        grid=(nb, s // tm),
        in_specs=[
            pl.BlockSpec((1, D_MIX, tm), lambda b, i: (b, 0, i)),
            tok,
            pl.BlockSpec((1, 1, N_MOD * D_MODEL), lambda b, i: (b, 0, 0)),
            lay(1, D_MIX),
            lay(D_MIX, D_MODEL),
            lay(1, D_MODEL),
        ],
        out_specs=(tok, tok),
        out_shape=(jax.ShapeDtypeStruct((nb, s, D_MODEL), F32), jax.ShapeDtypeStruct((nb, s, D_MODEL), BF16)),
        compiler_params=pltpu.CompilerParams(
            dimension_semantics=("arbitrary", "arbitrary"), vmem_limit_bytes=VMEM_LIMIT),
        name="attn_out",
    )(o_t, x, mod, w["g_out"], w["w_o"], w["g_ffn"])


def _ffn_kernel(hm_ref, hp_ref, hn_ref, x1_ref, mod_ref, wup_ref, cw_ref, cb_ref, wdn_ref, gfin_ref,
                out_ref, hext_ref, *, final):
    i = pl.program_id(1)
    c = pl.program_id(2)
    n_i = pl.num_programs(1)
    n_c = pl.num_programs(2)
    tm = hm_ref.shape[1]

    @pl.when(c == 0)
    def _():
        keep_prev = jnp.where(i > 0, 1.0, 0.0).astype(F32)
        keep_next = jnp.where(i < n_i - 1, 1.0, 0.0).astype(F32)
        hext_ref[0:HALO, :] = (hp_ref[0, 0].astype(F32) * keep_prev).astype(BF16)
        hext_ref[HALO:HALO + tm, :] = hm_ref[0]
        hext_ref[HALO + tm:HALO + tm + HALO, :] = (hn_ref[0, 0].astype(F32) * keep_next).astype(BF16)
        out_ref[0] = jnp.zeros((tm, D_MODEL), F32)

    u = jnp.dot(hext_ref[...], wup_ref[0, 0], preferred_element_type=F32)
    n_ext = tm + 2 * HALO
    u_prev = pltpu.roll(u, 1, 0)[HALO:HALO + tm]
    u_next = pltpu.roll(u, n_ext - 1, 0)[HALO:HALO + tm]
    u_mid = u[HALO:HALO + tm]
    cw = cw_ref[0, 0]
    y = u_prev * cw[0:1] + u_mid * cw[1:2] + u_next * cw[2:3] + cb_ref[0, 0]
    val = y[:, :FF_CHUNK]
    gate = y[:, FF_CHUNK:]
    act = (gate * (1.0 / (1.0 + jnp.exp(-gate)))) * val
    out_ref[0] += jnp.dot(act.astype(BF16), wdn_ref[0, 0], preferred_element_type=F32)

    @pl.when(c == n_c - 1)
    def _():
        gate_f = mod_ref[0][:, 5 * D_MODEL:6 * D_MODEL]
        x2 = x1_ref[0] + gate_f * out_ref[0]
        if final:
            r = _rsqrt_mean(jnp.sum(x2 * x2, axis=-1, keepdims=True), D_MODEL)
            x2 = (x2 * r) * gfin_ref[...]
        out_ref[0] = x2


def _ffn(h2, x1, mod, l, w, g_final, tm, final):
    nb, s, _ = x1.shape
    n_i = s // tm
    hb = tm // HALO
    h2_blocks = h2.reshape(nb, s // HALO, HALO, D_MODEL)
    tok = pl.BlockSpec((1, tm, D_MODEL), lambda b, i, c: (b, i, 0))
    chunk = lambda *blk: pl.BlockSpec((1, 1) + blk, lambda b, i, c: (l, c) + (0,) * len(blk))
    return pl.pallas_call(
        functools.partial(_ffn_kernel, final=final),
        grid=(nb, n_i, N_FF_CHUNKS),
        in_specs=[
            tok,
            pl.BlockSpec((1, 1, HALO, D_MODEL), lambda b, i, c: (b, jnp.maximum(i * hb - 1, 0), 0, 0)),
            pl.BlockSpec((1, 1, HALO, D_MODEL),
                         lambda b, i, c: (b, jnp.minimum((i + 1) * hb, s // HALO - 1), 0, 0)),
            tok,
            pl.BlockSpec((1, 1, N_MOD * D_MODEL), lambda b, i, c: (b, 0, 0)),
            chunk(D_MODEL, 2 * FF_CHUNK),
            chunk(3, 2 * FF_CHUNK),
            chunk(1, 2 * FF_CHUNK),
            chunk(FF_CHUNK, D_MODEL),
            pl.BlockSpec((1, D_MODEL), lambda b, i, c: (0, 0)),
        ],
        out_specs=tok,
        out_shape=jax.ShapeDtypeStruct((nb, s, D_MODEL), F32),
        scratch_shapes=[pltpu.VMEM((tm + 2 * HALO, D_MODEL), BF16)],
        compiler_params=pltpu.CompilerParams(
            dimension_semantics=("arbitrary", "arbitrary", "arbitrary"), vmem_limit_bytes=VMEM_LIMIT),
        name="ffn",
    )(h2, h2_blocks, h2_blocks, x1, mod, w["w_up"], w["conv_w"], w["conv_b"], w["w_down"], g_final)


def _rope_tables_t(n_tokens, rot_dim):
    rows = n_tokens // GRID_W
    row = jnp.repeat(jnp.arange(rows, dtype=F32), GRID_W)
    col = jnp.tile(jnp.arange(GRID_W, dtype=F32), rows)
    quarter = rot_dim // 4
    inv_freq = ROPE_THETA ** (-jnp.arange(quarter, dtype=F32) / quarter)
    ang = jnp.stack([row, col], axis=-1)[:, :, None] * inv_freq
    cos, sin = jnp.cos(ang), jnp.sin(ang)
    c = jnp.concatenate([cos[:, 0], cos[:, 0], cos[:, 1], cos[:, 1]], axis=-1).T
    s = jnp.concatenate([-sin[:, 0], sin[:, 0], -sin[:, 1], sin[:, 1]], axis=-1).T
    return c, s


def _prepare_weights(g_attn, w_in, g_q_a, g_k_a, g_cq, w_uq, g_ckv, w_ukv, g_out_a, g_out_b, w_o,
                     g_ffn, w_up, conv_w, conv_b, w_down, tm):
    o_kr = D_IN - QK_ROPE_DIM
    w_in_p = jnp.concatenate([
        w_in[..., :o_kr], jnp.zeros((DEPTH, D_MODEL, ROPE_SLOT_OFF), F32), w_in[..., o_kr:],
        jnp.zeros((DEPTH, D_MODEL, LANES - ROPE_SLOT_OFF - QK_ROPE_DIM), F32)], axis=-1).astype(BF16)
    hq = QK_NOPE_DIM + QK_ROPE_DIM
    w_uq_p = jnp.pad(w_uq.reshape(DEPTH, Q_LORA, N_HEADS_B, hq), ((0, 0), (0, 0), (0, 0), (0, HEAD_SLOT - hq)))
    w_uq_t = w_uq_p.reshape(DEPTH, Q_LORA, N_HEADS_B * HEAD_SLOT).transpose(0, 2, 1).astype(BF16)
    w_ukv_h = w_ukv.reshape(DEPTH, KV_LORA, N_HEADS_B, QK_NOPE_DIM + V_DIM_B)
    w_uk = jnp.pad(w_ukv_h[..., :QK_NOPE_DIM], ((0, 0), (0, 0), (0, 0), (0, HEAD_SLOT - QK_NOPE_DIM)))
    w_uk = w_uk.reshape(DEPTH, KV_LORA, N_HEADS_B * HEAD_SLOT).astype(BF16)
    w_uv_t = w_ukv_h[..., QK_NOPE_DIM:].reshape(DEPTH, KV_LORA, WIDTH_B).transpose(0, 2, 1).astype(BF16)
    halves = lambda a: jnp.stack([a[..., :D_FF].reshape(a.shape[:-1] + (N_FF_CHUNKS, FF_CHUNK)),
                                  a[..., D_FF:].reshape(a.shape[:-1] + (N_FF_CHUNKS, FF_CHUNK))], axis=-2)
    w_up_c = halves(w_up).transpose(0, 2, 1, 3, 4).reshape(DEPTH, N_FF_CHUNKS, D_MODEL, 2 * FF_CHUNK).astype(BF16)
    conv_w_c = halves(conv_w).transpose(0, 2, 1, 3, 4).reshape(DEPTH, N_FF_CHUNKS, 3, 2 * FF_CHUNK)
    conv_b_c = halves(conv_b).reshape(DEPTH, N_FF_CHUNKS, 1, 2 * FF_CHUNK)
    w_down_c = w_down.reshape(DEPTH, N_FF_CHUNKS, FF_CHUNK, D_MODEL).astype(BF16)
    col = lambda g: jnp.broadcast_to(g[:, :, None], g.shape + (tm,))
    return {
        "g_attn": g_attn.reshape(DEPTH, 1, D_MODEL), "w_in": w_in_p,
        "gq_t": col(g_q_a), "gk_t": col(g_k_a), "gcq_t": col(g_cq), "w_uq_t": w_uq_t,
        "g_ckv": g_ckv.reshape(DEPTH, 1, KV_LORA), "w_uk": w_uk, "w_uv_t": w_uv_t,
        "g_out": jnp.concatenate([g_out_a, g_out_b], axis=-1).reshape(DEPTH, 1, D_MIX),
        "w_o": w_o.astype(BF16), "g_ffn": g_ffn.reshape(DEPTH, 1, D_MODEL),
        "w_up": w_up_c, "conv_w": conv_w_c, "conv_b": conv_b_c, "w_down": w_down_c,
    }


def _tiles(s):
    return min(512, s), min(256, s), min(1024, s)


def kernel(x_prompt, x_sample, c_prompt, c_sample, w_ada, b_ada, g_attn, w_in, g_q_a, g_k_a, g_cq, w_uq,
           g_ckv, w_ukv, g_out_a, g_out_b, w_o, g_ffn, w_up, conv_w, conv_b, w_down, g_final):
    n_prompt = x_prompt.shape[0]
    s = x_prompt.shape[1]
    assert x_sample.shape[1] == s and s % GRID_W == 0
    tm, tq, tf = _tiles(s)
    assert s % tm == 0 and s % tq == 0 and s % tf == 0 and tf % HALO == 0
    x = jnp.concatenate([x_prompt, x_sample], axis=0)
    c = jnp.concatenate([c_prompt, c_sample], axis=0)
    nb = x.shape[0]

    w = _prepare_weights(g_attn, w_in, g_q_a, g_k_a, g_cq, w_uq, g_ckv, w_ukv, g_out_a, g_out_b, w_o,
                         g_ffn, w_up, conv_w, conv_b, w_down, tm)
    ca, sa = _rope_tables_t(s, HEAD_DIM_A)
    cb, sb = _rope_tables_t(s, QK_ROPE_DIM)
    tabs = {"ca": ca, "sa": sa, "cb": cb, "sb": sb}
    mod_all = _modulation(c, w_ada, b_ada)
    g_fin = g_final.reshape(1, D_MODEL)

    for l in range(DEPTH):
        mod = mod_all[l].reshape(nb, 1, N_MOD * D_MODEL)
        qat, ka, vat, qbt, kb, vbt = _attn_in(x, mod, l, w, tabs, tm)
        o_t = _attention(qat, ka, vat, qbt, kb, vbt, tq)
        x1, h2 = _attn_out(o_t, x, mod, l, w, tm)
        x = _ffn(h2, x1, mod, l, w, g_fin, tf, final=(l == DEPTH - 1))
    return x[:n_prompt], x[n_prompt:]
```

```python
import functools
import math

import jax
import jax.numpy as jnp
from jax import lax
from jax.experimental import pallas as pl
from jax.experimental.pallas import tpu as pltpu

F32 = jnp.float32
BF16 = jnp.bfloat16

D_MODEL = 1024
DEPTH = 4
GRID_W = 64
ROPE_THETA = 10000.0
EPS = 1e-6
N_HEADS_A = 8
N_KV_HEADS_A = 2
HEAD_DIM_A = 64
N_HEADS_B = 8
QK_NOPE_DIM = 64
QK_ROPE_DIM = 32
V_DIM_B = 64
Q_LORA = 256
KV_LORA = 128
WIDTH_A = N_HEADS_A * HEAD_DIM_A
WIDTH_B = N_HEADS_B * V_DIM_B
D_MIX = WIDTH_A + WIDTH_B
KV_WIDTH_A = N_KV_HEADS_A * HEAD_DIM_A
D_IN = WIDTH_A + 2 * KV_WIDTH_A + Q_LORA + KV_LORA + QK_ROPE_DIM
D_FF = 2816
N_MOD = 6

LANES = 128
HEAD_SLOT = 128
D_IN_PAD = 1280
ROPE_SLOT_OFF = 64
FF_CHUNK = 256
N_FF_CHUNKS = D_FF // FF_CHUNK
HALO = 16
N_HEADS = N_HEADS_A + N_HEADS_B
N_KEY_SLABS = 1 + N_HEADS_B
V_ROWS = KV_WIDTH_A + WIDTH_B
HEAD_DV = HEAD_DIM_A
assert HEAD_DV == V_DIM_B
KEY_CHUNK = 256
FOLD_ROWS = 16
LOG2E = math.log2(math.e)
VMEM_LIMIT = 56 * 1024 * 1024


def _rsqrt_mean(ss, n):
    return lax.rsqrt(ss * (1.0 / n) + EPS)


def _mod_kernel(c_ref, w_ref, b_ref, o_ref):
    c = c_ref[...]
    ca = c * (1.0 / (1.0 + jnp.exp(-c)))
    o_ref[0] = jnp.dot(ca.astype(BF16), w_ref[0].astype(BF16), preferred_element_type=F32) + b_ref[0]


def _modulation(c, w_ada, b_ada):
    nb = c.shape[0]
    n_col = N_MOD * D_MODEL
    tn = 1024
    return pl.pallas_call(
        _mod_kernel,
        grid=(DEPTH, n_col // tn),
        in_specs=[
            pl.BlockSpec((nb, D_MODEL), lambda l, j: (0, 0)),
            pl.BlockSpec((1, D_MODEL, tn), lambda l, j: (l, 0, j)),
            pl.BlockSpec((1, 1, tn), lambda l, j: (l, 0, j)),
        ],
        out_specs=pl.BlockSpec((1, nb, tn), lambda l, j: (l, 0, j)),
        out_shape=jax.ShapeDtypeStruct((DEPTH, nb, n_col), F32),
        compiler_params=pltpu.CompilerParams(dimension_semantics=("arbitrary", "arbitrary")),
        name="modulation",
    )(c, w_ada, b_ada.reshape(DEPTH, 1, n_col))


def _rope_t(x, c, s, q):
    sw = jnp.concatenate([x[q:2 * q], x[0:q], x[3 * q:4 * q], x[2 * q:3 * q]], axis=0)
    return x * c + sw * s


def _attn_in_kernel(x_ref, mod_ref, gat_ref, win_ref, gq_ref, gk_ref, gcq_ref, wuqt_ref, gckv_ref,
                    wuk_ref, wuvt_ref, ca_ref, sa_ref, cb_ref, sb_ref,
                    q_ref, k_ref, v_ref):
    x = x_ref[0]
    tm = x.shape[0]
    mod = mod_ref[0]
    shift = mod[:, 0:D_MODEL]
    scale = mod[:, D_MODEL:2 * D_MODEL]
    r = _rsqrt_mean(jnp.sum(x * x, axis=-1, keepdims=True), D_MODEL)
    h = (x * r) * gat_ref[0] * (1.0 + scale) + shift
    proj = jnp.dot(h.astype(BF16), win_ref[0], preferred_element_type=F32)

    ca = ca_ref[...]
    sa = sa_ref[...]
    cb = cb_ref[...]
    sb = sb_ref[...]
    zeros_half = jnp.zeros((HEAD_DIM_A, tm), F32)

    o_qa = 0
    q_at = proj[:, o_qa:o_qa + WIDTH_A].T
    qscale_a = (HEAD_DIM_A ** -0.5) * LOG2E
    gq = gq_ref[0]
    for j in range(N_HEADS_A):
        xh = q_at[HEAD_DIM_A * j:HEAD_DIM_A * (j + 1)]
        rh = _rsqrt_mean(jnp.sum(xh * xh, axis=0, keepdims=True), HEAD_DIM_A)
        xr = _rope_t((xh * rh) * gq, ca, sa, HEAD_DIM_A // 4) * qscale_a
        g = j // (N_HEADS_A // N_KV_HEADS_A)
        slot = jnp.concatenate([xr, zeros_half] if g == 0 else [zeros_half, xr], axis=0)
        q_ref[0, HEAD_SLOT * j:HEAD_SLOT * (j + 1), :] = slot.astype(BF16)

    o_ka = WIDTH_A
    k_at = proj[:, o_ka:o_ka + KV_WIDTH_A].T
    gk = gk_ref[0]
    kparts = []
    for g in range(N_KV_HEADS_A):
        xh = k_at[HEAD_DIM_A * g:HEAD_DIM_A * (g + 1)]
        rh = _rsqrt_mean(jnp.sum(xh * xh, axis=0, keepdims=True), HEAD_DIM_A)
        kparts.append(_rope_t((xh * rh) * gk, ca, sa, HEAD_DIM_A // 4))
    k_ref[0, 0] = jnp.concatenate(kparts, axis=0).T.astype(BF16)
    o_va = o_ka + KV_WIDTH_A
    v_ref[0, 0:KV_WIDTH_A, :] = proj[:, o_va:o_va + KV_WIDTH_A].T.astype(BF16)

    o_cq = o_va + KV_WIDTH_A
    cq_t = proj[:, o_cq:o_cq + Q_LORA].T
    rq = _rsqrt_mean(jnp.sum(cq_t * cq_t, axis=0, keepdims=True), Q_LORA)
    cqn = (cq_t * rq) * gcq_ref[0]
    qb_t = jnp.dot(wuqt_ref[0], cqn.astype(BF16), preferred_element_type=F32)
    qscale_b = ((QK_NOPE_DIM + QK_ROPE_DIM) ** -0.5) * LOG2E
    zeros_pad = jnp.zeros((HEAD_SLOT - QK_NOPE_DIM - QK_ROPE_DIM, tm), F32)
    for j in range(N_HEADS_B):
        base = HEAD_SLOT * j
        nope = qb_t[base:base + QK_NOPE_DIM]
        rope = _rope_t(qb_t[base + ROPE_SLOT_OFF:base + ROPE_SLOT_OFF + QK_ROPE_DIM], cb, sb, QK_ROPE_DIM // 4)
        slot = jnp.concatenate([nope, rope, zeros_pad], axis=0) * qscale_b
        q_ref[0, N_HEADS_A * HEAD_SLOT + base:N_HEADS_A * HEAD_SLOT + base + HEAD_SLOT, :] = slot.astype(BF16)

    o_ckv = o_cq + Q_LORA
    ckv = proj[:, o_ckv:o_ckv + KV_LORA]
    rkv = _rsqrt_mean(jnp.sum(ckv * ckv, axis=-1, keepdims=True), KV_LORA)
    ckvn = (ckv * rkv) * gckv_ref[0]
    k_nope = jnp.dot(ckvn.astype(BF16), wuk_ref[0], preferred_element_type=F32)
    v_ref[0, KV_WIDTH_A:KV_WIDTH_A + WIDTH_B, :] = jnp.dot(
        wuvt_ref[0], ckvn.T.astype(BF16), preferred_element_type=F32).astype(BF16)
    o_kr = o_ckv + KV_LORA
    kr_t = proj[:, o_kr:o_kr + LANES].T
    kr_rope = _rope_t(kr_t[ROPE_SLOT_OFF:ROPE_SLOT_OFF + QK_ROPE_DIM], cb, sb, QK_ROPE_DIM // 4)
    kr_tile = jnp.concatenate(
        [jnp.zeros((ROPE_SLOT_OFF, tm), F32), kr_rope, zeros_pad], axis=0).T
    for j in range(N_HEADS_B):
        base = HEAD_SLOT * j
        k_ref[0, 1 + j] = (k_nope[:, base:base + HEAD_SLOT] + kr_tile).astype(BF16)


def _attn_in(x, mod, l, w, tabs, tm):
    nb, s, _ = x.shape
    grid = (nb, s // tm)
    lay = lambda *blk: pl.BlockSpec((1,) + blk, lambda b, i: (l,) + (0,) * len(blk))
    tab = lambda rows: pl.BlockSpec((rows, tm), lambda b, i: (0, i))
    out_shapes = (
        jax.ShapeDtypeStruct((nb, N_HEADS * HEAD_SLOT, s), BF16),
        jax.ShapeDtypeStruct((nb, N_KEY_SLABS, s, HEAD_SLOT), BF16),
        jax.ShapeDtypeStruct((nb, V_ROWS, s), BF16),
    )
    return pl.pallas_call(
        _attn_in_kernel,
        grid=grid,
        in_specs=[
            pl.BlockSpec((1, tm, D_MODEL), lambda b, i: (b, i, 0)),
            pl.BlockSpec((1, 1, N_MOD * D_MODEL), lambda b, i: (b, 0, 0)),
            lay(1, D_MODEL),
            lay(D_MODEL, D_IN_PAD),
            lay(HEAD_DIM_A, tm),
            lay(HEAD_DIM_A, tm),
            lay(Q_LORA, tm),
            lay(N_HEADS_B * HEAD_SLOT, Q_LORA),
            lay(1, KV_LORA),
            lay(KV_LORA, N_HEADS_B * HEAD_SLOT),
            lay(WIDTH_B, KV_LORA),
            tab(HEAD_DIM_A), tab(HEAD_DIM_A), tab(QK_ROPE_DIM), tab(QK_ROPE_DIM),
        ],
        out_specs=(pl.BlockSpec((1, N_HEADS * HEAD_SLOT, tm), lambda b, i: (b, 0, i)),
                   pl.BlockSpec((1, N_KEY_SLABS, tm, HEAD_SLOT), lambda b, i: (b, 0, i, 0)),
                   pl.BlockSpec((1, V_ROWS, tm), lambda b, i: (b, 0, i))),
        out_shape=out_shapes,
        compiler_params=pltpu.CompilerParams(
            dimension_semantics=("arbitrary", "arbitrary"), vmem_limit_bytes=VMEM_LIMIT),
        name="attn_in",
    )(x, mod, w["g_attn"], w["w_in"], w["gq_t"], w["gk_t"], w["gcq_t"], w["w_uq_t"], w["g_ckv"],
      w["w_uk"], w["w_uv_t"], tabs["ca"], tabs["sa"], tabs["cb"], tabs["sb"])


def _fold_rows(x, op):
    parts = [x[FOLD_ROWS * i:FOLD_ROWS * (i + 1)] for i in range(x.shape[0] // FOLD_ROWS)]
    while len(parts) > 1:
        parts = [op(parts[2 * i], parts[2 * i + 1]) for i in range(len(parts) // 2)]
    return parts[0]


def _key_slab(h):
    if isinstance(h, int):
        return 0 if h < N_HEADS_A else h - N_HEADS_A + 1
    return jnp.where(h < N_HEADS_A, 0, h - N_HEADS_A + 1)


def _value_row(h):
    per_kv = N_HEADS_A // N_KV_HEADS_A
    if isinstance(h, int):
        return HEAD_DV * (h // per_kv if h < N_HEADS_A else h - N_HEADS_A + N_KV_HEADS_A)
    return HEAD_DV * jnp.where(h < N_HEADS_A, h // per_kv, h - N_HEADS_A + N_KV_HEADS_A)


def _row_block(start, size):
    if isinstance(start, int):
        return pl.ds(start, size)
    return pl.ds(pl.multiple_of(start, size), size)


def _attention_kernel(q_ref, k_ref, v_ref, ot_ref, s0_ref, s1_ref):
    n_keys = k_ref.shape[2]
    tq = ot_ref.shape[2]
    n_chunks = n_keys // KEY_CHUNK
    rows = lambda c: slice(KEY_CHUNK * c, KEY_CHUNK * (c + 1))

    def stage(h_next, s_next_ref, h_cur, s_cur_ref, m_cur):
        m_next = None
        if h_next is not None:
            q_t = q_ref[0, _row_block(h_next * HEAD_SLOT, HEAD_SLOT), :]
            slab = _key_slab(h_next)
        if h_cur is not None:
            m_b = jnp.broadcast_to(jnp.max(m_cur, axis=0, keepdims=True), (KEY_CHUNK, tq))
            v_rows = _row_block(_value_row(h_cur), HEAD_DV)
            l_acc = jnp.zeros((FOLD_ROWS, tq), F32)
            o_acc = jnp.zeros((HEAD_DV, tq), F32)
        for c in range(n_chunks):
            if h_next is not None:
                s = jnp.dot(k_ref[0, slab, rows(c), :], q_t, preferred_element_type=F32)
                s_next_ref[rows(c), :] = s
                folded = _fold_rows(s, jnp.maximum)
                m_next = folded if m_next is None else jnp.maximum(m_next, folded)
            if h_cur is not None:
                p = jnp.exp2(s_cur_ref[rows(c), :] - m_b)
                l_acc = l_acc + _fold_rows(p, jnp.add)
                o_acc = o_acc + jnp.dot(v_ref[0, v_rows, rows(c)], p.astype(BF16),
                                        preferred_element_type=F32)
        if h_cur is not None:
            denom = jnp.sum(l_acc, axis=0, keepdims=True)
            ot_ref[0, _row_block(h_cur * HEAD_DV, HEAD_DV), :] = o_acc * (1.0 / denom)
        return m_next

    def pair(i, m_even):
        h = 2 * i
        m_odd = stage(h + 1, s1_ref, h, s0_ref, m_even)
        return stage(h + 2, s0_ref, h + 1, s1_ref, m_odd)

    assert N_HEADS % 2 == 0
    m_even = lax.fori_loop(0, N_HEADS // 2 - 1, pair, stage(0, s0_ref, None, None, None))
    m_odd = stage(N_HEADS - 1, s1_ref, N_HEADS - 2, s0_ref, m_even)
    stage(None, None, N_HEADS - 1, s1_ref, m_odd)


def _attention(q_all, k_all, v_all, tq):
    nb, _, s, _ = k_all.shape
    assert s % KEY_CHUNK == 0
    return pl.pallas_call(
        _attention_kernel,
        grid=(nb, s // tq),
        in_specs=[pl.BlockSpec((1, N_HEADS * HEAD_SLOT, tq), lambda b, i: (b, 0, i)),
                  pl.BlockSpec((1, N_KEY_SLABS, s, HEAD_SLOT), lambda b, i: (b, 0, 0, 0)),
                  pl.BlockSpec((1, V_ROWS, s), lambda b, i: (b, 0, 0))],
        out_specs=pl.BlockSpec((1, D_MIX, tq), lambda b, i: (b, 0, i)),
        out_shape=jax.ShapeDtypeStruct((nb, D_MIX, s), F32),
        scratch_shapes=[pltpu.VMEM((s, tq), F32), pltpu.VMEM((s, tq), F32)],
        compiler_params=pltpu.CompilerParams(
            dimension_semantics=("arbitrary", "arbitrary"), vmem_limit_bytes=VMEM_LIMIT),
        name="attention",
    )(q_all, k_all, v_all)


def _attn_out_kernel(ot_ref, x_ref, mod_ref, go_ref, wo_ref, gffn_ref, x1_ref, h2_ref):
    o_t = ot_ref[0]
    oa = o_t[:WIDTH_A]
    ob = o_t[WIDTH_A:]
    ra = _rsqrt_mean(jnp.sum(oa * oa, axis=0, keepdims=True), WIDTH_A)
    rb = _rsqrt_mean(jnp.sum(ob * ob, axis=0, keepdims=True), WIDTH_B)
    on = jnp.concatenate([oa * ra, ob * rb], axis=0).T * go_ref[0]
    res = jnp.dot(on.astype(BF16), wo_ref[0], preferred_element_type=F32)
    mod = mod_ref[0]
    gate_a = mod[:, 2 * D_MODEL:3 * D_MODEL]
    shift_f = mod[:, 3 * D_MODEL:4 * D_MODEL]
    scale_f = mod[:, 4 * D_MODEL:5 * D_MODEL]
    x1 = x_ref[0] + gate_a * res
    x1_ref[0] = x1
    r = _rsqrt_mean(jnp.sum(x1 * x1, axis=-1, keepdims=True), D_MODEL)
    h2_ref[0] = ((x1 * r) * gffn_ref[0] * (1.0 + scale_f) + shift_f).astype(BF16)


def _attn_out(o_t, x, mod, l, w, tm):
    nb, s, _ = x.shape
    lay = lambda *blk: pl.BlockSpec((1,) + blk, lambda b, i: (l,) + (0,) * len(blk))
    tok = pl.BlockSpec((1, tm, D_MODEL), lambda b, i: (b, i, 0))
    return pl.pallas_call(
        _attn_out_kernel,
        grid=(nb, s // tm),
        in_specs=[
            pl.BlockSpec((1, D_MIX, tm), lambda b, i: (b, 0, i)),
            tok,
            pl.BlockSpec((1, 1, N_MOD * D_MODEL), lambda b, i: (b, 0, 0)),
            lay(1, D_MIX),
            lay(D_MIX, D_MODEL),
            lay(1, D_MODEL),
        ],
        out_specs=(tok, tok),
        out_shape=(jax.ShapeDtypeStruct((nb, s, D_MODEL), F32), jax.ShapeDtypeStruct((nb, s, D_MODEL), BF16)),
        compiler_params=pltpu.CompilerParams(
            dimension_semantics=("arbitrary", "arbitrary"), vmem_limit_bytes=VMEM_LIMIT),
        name="attn_out",
    )(o_t, x, mod, w["g_out"], w["w_o"], w["g_ffn"])


def _ffn_kernel(hm_ref, hp_ref, hn_ref, x1_ref, mod_ref, wup_ref, cw_ref, cb_ref, wdn_ref, gfin_ref,
                out_ref, hext_ref, *, final):
    i = pl.program_id(1)
    c = pl.program_id(2)
    n_i = pl.num_programs(1)
    n_c = pl.num_programs(2)
    tm = hm_ref.shape[1]

    @pl.when(c == 0)
    def _():
        keep_prev = jnp.where(i > 0, 1.0, 0.0).astype(F32)
        keep_next = jnp.where(i < n_i - 1, 1.0, 0.0).astype(F32)
        hext_ref[0:HALO, :] = (hp_ref[0, 0].astype(F32) * keep_prev).astype(BF16)
        hext_ref[HALO:HALO + tm, :] = hm_ref[0]
        hext_ref[HALO + tm:HALO + tm + HALO, :] = (hn_ref[0, 0].astype(F32) * keep_next).astype(BF16)
        out_ref[0] = jnp.zeros((tm, D_MODEL), F32)

    u = jnp.dot(hext_ref[...], wup_ref[0, 0], preferred_element_type=F32)
    n_ext = tm + 2 * HALO
    u_prev = pltpu.roll(u, 1, 0)[HALO:HALO + tm]
    u_next = pltpu.roll(u, n_ext - 1, 0)[HALO:HALO + tm]
    u_mid = u[HALO:HALO + tm]
    cw = cw_ref[0, 0]
    y = u_prev * cw[0:1] + u_mid * cw[1:2] + u_next * cw[2:3] + cb_ref[0, 0]
    val = y[:, :FF_CHUNK]
    gate = y[:, FF_CHUNK:]
    act = (gate * (1.0 / (1.0 + jnp.exp(-gate)))) * val
    out_ref[0] += jnp.dot(act.astype(BF16), wdn_ref[0, 0], preferred_element_type=F32)

    @pl.when(c == n_c - 1)
    def _():
        gate_f = mod_ref[0][:, 5 * D_MODEL:6 * D_MODEL]
        x2 = x1_ref[0] + gate_f * out_ref[0]
        if final:
            r = _rsqrt_mean(jnp.sum(x2 * x2, axis=-1, keepdims=True), D_MODEL)
            x2 = (x2 * r) * gfin_ref[...]
        out_ref[0] = x2


def _ffn(h2, x1, mod, l, w, g_final, tm, final):
    nb, s, _ = x1.shape
    n_i = s // tm
    hb = tm // HALO
    h2_blocks = h2.reshape(nb, s // HALO, HALO, D_MODEL)
    tok = pl.BlockSpec((1, tm, D_MODEL), lambda b, i, c: (b, i, 0))
    chunk = lambda *blk: pl.BlockSpec((1, 1) + blk, lambda b, i, c: (l, c) + (0,) * len(blk))
    return pl.pallas_call(
        functools.partial(_ffn_kernel, final=final),
        grid=(nb, n_i, N_FF_CHUNKS),
        in_specs=[
            tok,
            pl.BlockSpec((1, 1, HALO, D_MODEL), lambda b, i, c: (b, jnp.maximum(i * hb - 1, 0), 0, 0)),
            pl.BlockSpec((1, 1, HALO, D_MODEL),
                         lambda b, i, c: (b, jnp.minimum((i + 1) * hb, s // HALO - 1), 0, 0)),
            tok,
            pl.BlockSpec((1, 1, N_MOD * D_MODEL), lambda b, i, c: (b, 0, 0)),
            chunk(D_MODEL, 2 * FF_CHUNK),
            chunk(3, 2 * FF_CHUNK),
            chunk(1, 2 * FF_CHUNK),
            chunk(FF_CHUNK, D_MODEL),
            pl.BlockSpec((1, D_MODEL), lambda b, i, c: (0, 0)),
        ],
        out_specs=tok,
        out_shape=jax.ShapeDtypeStruct((nb, s, D_MODEL), F32),
        scratch_shapes=[pltpu.VMEM((tm + 2 * HALO, D_MODEL), BF16)],
        compiler_params=pltpu.CompilerParams(
            dimension_semantics=("arbitrary", "arbitrary", "arbitrary"), vmem_limit_bytes=VMEM_LIMIT),
        name="ffn",
    )(h2, h2_blocks, h2_blocks, x1, mod, w["w_up"], w["conv_w"], w["conv_b"], w["w_down"], g_final)


def _rope_tables_t(n_tokens, rot_dim):
    rows = n_tokens // GRID_W
    row = jnp.repeat(jnp.arange(rows, dtype=F32), GRID_W)
    col = jnp.tile(jnp.arange(GRID_W, dtype=F32), rows)
    quarter = rot_dim // 4
    inv_freq = ROPE_THETA ** (-jnp.arange(quarter, dtype=F32) / quarter)
    ang = jnp.stack([row, col], axis=-1)[:, :, None] * inv_freq
    cos, sin = jnp.cos(ang), jnp.sin(ang)
    c = jnp.concatenate([cos[:, 0], cos[:, 0], cos[:, 1], cos[:, 1]], axis=-1).T
    s = jnp.concatenate([-sin[:, 0], sin[:, 0], -sin[:, 1], sin[:, 1]], axis=-1).T
    return c, s


def _prepare_weights(g_attn, w_in, g_q_a, g_k_a, g_cq, w_uq, g_ckv, w_ukv, g_out_a, g_out_b, w_o,
                     g_ffn, w_up, conv_w, conv_b, w_down, tm):
    o_kr = D_IN - QK_ROPE_DIM
    w_in_p = jnp.concatenate([
        w_in[..., :o_kr], jnp.zeros((DEPTH, D_MODEL, ROPE_SLOT_OFF), F32), w_in[..., o_kr:],
        jnp.zeros((DEPTH, D_MODEL, LANES - ROPE_SLOT_OFF - QK_ROPE_DIM), F32)], axis=-1).astype(BF16)
    hq = QK_NOPE_DIM + QK_ROPE_DIM
    w_uq_p = jnp.pad(w_uq.reshape(DEPTH, Q_LORA, N_HEADS_B, hq), ((0, 0), (0, 0), (0, 0), (0, HEAD_SLOT - hq)))
    w_uq_t = w_uq_p.reshape(DEPTH, Q_LORA, N_HEADS_B * HEAD_SLOT).transpose(0, 2, 1).astype(BF16)
    w_ukv_h = w_ukv.reshape(DEPTH, KV_LORA, N_HEADS_B, QK_NOPE_DIM + V_DIM_B)
    w_uk = jnp.pad(w_ukv_h[..., :QK_NOPE_DIM], ((0, 0), (0, 0), (0, 0), (0, HEAD_SLOT - QK_NOPE_DIM)))
    w_uk = w_uk.reshape(DEPTH, KV_LORA, N_HEADS_B * HEAD_SLOT).astype(BF16)
    w_uv_t = w_ukv_h[..., QK_NOPE_DIM:].reshape(DEPTH, KV_LORA, WIDTH_B).transpose(0, 2, 1).astype(BF16)
    halves = lambda a: jnp.stack([a[..., :D_FF].reshape(a.shape[:-1] + (N_FF_CHUNKS, FF_CHUNK)),
                                  a[..., D_FF:].reshape(a.shape[:-1] + (N_FF_CHUNKS, FF_CHUNK))], axis=-2)
    w_up_c = halves(w_up).transpose(0, 2, 1, 3, 4).reshape(DEPTH, N_FF_CHUNKS, D_MODEL, 2 * FF_CHUNK).astype(BF16)
    conv_w_c = halves(conv_w).transpose(0, 2, 1, 3, 4).reshape(DEPTH, N_FF_CHUNKS, 3, 2 * FF_CHUNK)
    conv_b_c = halves(conv_b).reshape(DEPTH, N_FF_CHUNKS, 1, 2 * FF_CHUNK)
    w_down_c = w_down.reshape(DEPTH, N_FF_CHUNKS, FF_CHUNK, D_MODEL).astype(BF16)
    col = lambda g: jnp.broadcast_to(g[:, :, None], g.shape + (tm,))
    return {
        "g_attn": g_attn.reshape(DEPTH, 1, D_MODEL), "w_in": w_in_p,
        "gq_t": col(g_q_a), "gk_t": col(g_k_a), "gcq_t": col(g_cq), "w_uq_t": w_uq_t,
        "g_ckv": g_ckv.reshape(DEPTH, 1, KV_LORA), "w_uk": w_uk, "w_uv_t": w_uv_t,
        "g_out": jnp.concatenate([g_out_a, g_out_b], axis=-1).reshape(DEPTH, 1, D_MIX),
        "w_o": w_o.astype(BF16), "g_ffn": g_ffn.reshape(DEPTH, 1, D_MODEL),
        "w_up": w_up_c, "conv_w": conv_w_c, "conv_b": conv_b_c, "w_down": w_down_c,
    }


def _tiles(s):
    return min(512, s), min(512, s), min(1024, s)


def kernel(x_prompt, x_sample, c_prompt, c_sample, w_ada, b_ada, g_attn, w_in, g_q_a, g_k_a, g_cq, w_uq,
           g_ckv, w_ukv, g_out_a, g_out_b, w_o, g_ffn, w_up, conv_w, conv_b, w_down, g_final):
    n_prompt = x_prompt.shape[0]
    s = x_prompt.shape[1]
    assert x_sample.shape[1] == s and s % GRID_W == 0
    tm, tq, tf = _tiles(s)
    assert s % tm == 0 and s % tq == 0 and s % tf == 0 and tf % HALO == 0
    x = jnp.concatenate([x_prompt, x_sample], axis=0)
    c = jnp.concatenate([c_prompt, c_sample], axis=0)
    nb = x.shape[0]

    w = _prepare_weights(g_attn, w_in, g_q_a, g_k_a, g_cq, w_uq, g_ckv, w_ukv, g_out_a, g_out_b, w_o,
                         g_ffn, w_up, conv_w, conv_b, w_down, tm)
    ca, sa = _rope_tables_t(s, HEAD_DIM_A)
    cb, sb = _rope_tables_t(s, QK_ROPE_DIM)
    tabs = {"ca": ca, "sa": sa, "cb": cb, "sb": sb}
    mod_all = _modulation(c, w_ada, b_ada)
    g_fin = g_final.reshape(1, D_MODEL)

    for l in range(DEPTH):
        mod = mod_all[l].reshape(nb, 1, N_MOD * D_MODEL)
        q_all, k_all, v_all = _attn_in(x, mod, l, w, tabs, tm)
        o_t = _attention(q_all, k_all, v_all, tq)
        x1, h2 = _attn_out(o_t, x, mod, l, w, tm)
        x = _ffn(h2, x1, mod, l, w, g_fin, tf, final=(l == DEPTH - 1))
    return x[:n_prompt], x[n_prompt:]
```

```python
import functools
import math

import jax
import jax.numpy as jnp
from jax import lax
from jax.experimental import pallas as pl
from jax.experimental.pallas import tpu as pltpu

F32 = jnp.float32
BF16 = jnp.bfloat16

D_MODEL = 1024
DEPTH = 4
GRID_W = 64
ROPE_THETA = 10000.0
EPS = 1e-6
N_HEADS_A = 8
N_KV_HEADS_A = 2
HEAD_DIM_A = 64
N_HEADS_B = 8
QK_NOPE_DIM = 64
QK_ROPE_DIM = 32
V_DIM_B = 64
Q_LORA = 256
KV_LORA = 128
WIDTH_A = N_HEADS_A * HEAD_DIM_A
WIDTH_B = N_HEADS_B * V_DIM_B
D_MIX = WIDTH_A + WIDTH_B
KV_WIDTH_A = N_KV_HEADS_A * HEAD_DIM_A
D_IN = WIDTH_A + 2 * KV_WIDTH_A + Q_LORA + KV_LORA + QK_ROPE_DIM
D_FF = 2816
N_MOD = 6

LANES = 128
HEAD_SLOT = 128
D_IN_PAD = 1280
ROPE_SLOT_OFF = 64
FF_CHUNK = 256
N_FF_CHUNKS = D_FF // FF_CHUNK
FF_SLABS = 2 * FF_CHUNK // LANES
HALO = 16
N_HEADS = N_HEADS_A + N_HEADS_B
N_KEY_SLABS = 1 + N_HEADS_B
V_ROWS = KV_WIDTH_A + WIDTH_B
HEAD_DV = HEAD_DIM_A
assert HEAD_DV == V_DIM_B
KEY_CHUNK = 256
FOLD_ROWS = 16
LOG2E = math.log2(math.e)
VMEM_LIMIT = 56 * 1024 * 1024


def _rsqrt_mean(ss, n):
    return lax.rsqrt(ss * (1.0 / n) + EPS)


def _mod_kernel(c_ref, w_ref, b_ref, o_ref):
    c = c_ref[...]
    ca = c * (1.0 / (1.0 + jnp.exp(-c)))
    o_ref[0] = jnp.dot(ca.astype(BF16), w_ref[0].astype(BF16), preferred_element_type=F32) + b_ref[0]


def _modulation(c, w_ada, b_ada):
    nb = c.shape[0]
    n_col = N_MOD * D_MODEL
    tn = 1024
    return pl.pallas_call(
        _mod_kernel,
        grid=(DEPTH, n_col // tn),
        in_specs=[
            pl.BlockSpec((nb, D_MODEL), lambda l, j: (0, 0)),
            pl.BlockSpec((1, D_MODEL, tn), lambda l, j: (l, 0, j)),
            pl.BlockSpec((1, 1, tn), lambda l, j: (l, 0, j)),
        ],
        out_specs=pl.BlockSpec((1, nb, tn), lambda l, j: (l, 0, j)),
        out_shape=jax.ShapeDtypeStruct((DEPTH, nb, n_col), F32),
        compiler_params=pltpu.CompilerParams(dimension_semantics=("arbitrary", "arbitrary")),
        name="modulation",
    )(c, w_ada, b_ada.reshape(DEPTH, 1, n_col))


def _rope_t(x, c, s, q):
    sw = jnp.concatenate([x[q:2 * q], x[0:q], x[3 * q:4 * q], x[2 * q:3 * q]], axis=0)
    return x * c + sw * s


def _attn_in_kernel(x_ref, mod_ref, gat_ref, win_ref, gq_ref, gk_ref, gcq_ref, wuqt_ref, gckv_ref,
                    wuk_ref, wuvt_ref, ca_ref, sa_ref, cb_ref, sb_ref,
                    q_ref, k_ref, v_ref):
    x = x_ref[0]
    tm = x.shape[0]
    mod = mod_ref[0]
    shift = mod[:, 0:D_MODEL]
    scale = mod[:, D_MODEL:2 * D_MODEL]
    r = _rsqrt_mean(jnp.sum(x * x, axis=-1, keepdims=True), D_MODEL)
    h = (x * r) * gat_ref[0] * (1.0 + scale) + shift
    proj = jnp.dot(h.astype(BF16), win_ref[0], preferred_element_type=F32)

    ca = ca_ref[...]
    sa = sa_ref[...]
    cb = cb_ref[...]
    sb = sb_ref[...]
    zeros_half = jnp.zeros((HEAD_DIM_A, tm), F32)

    o_qa = 0
    q_at = proj[:, o_qa:o_qa + WIDTH_A].T
    qscale_a = (HEAD_DIM_A ** -0.5) * LOG2E
    gq = gq_ref[0]
    for j in range(N_HEADS_A):
        xh = q_at[HEAD_DIM_A * j:HEAD_DIM_A * (j + 1)]
        rh = _rsqrt_mean(jnp.sum(xh * xh, axis=0, keepdims=True), HEAD_DIM_A)
        xr = _rope_t((xh * rh) * gq, ca, sa, HEAD_DIM_A // 4) * qscale_a
        g = j // (N_HEADS_A // N_KV_HEADS_A)
        slot = jnp.concatenate([xr, zeros_half] if g == 0 else [zeros_half, xr], axis=0)
        q_ref[0, HEAD_SLOT * j:HEAD_SLOT * (j + 1), :] = slot.astype(BF16)

    o_ka = WIDTH_A
    k_at = proj[:, o_ka:o_ka + KV_WIDTH_A].T
    gk = gk_ref[0]
    kparts = []
    for g in range(N_KV_HEADS_A):
        xh = k_at[HEAD_DIM_A * g:HEAD_DIM_A * (g + 1)]
        rh = _rsqrt_mean(jnp.sum(xh * xh, axis=0, keepdims=True), HEAD_DIM_A)
        kparts.append(_rope_t((xh * rh) * gk, ca, sa, HEAD_DIM_A // 4))
    k_ref[0, 0] = jnp.concatenate(kparts, axis=0).T.astype(BF16)
    o_va = o_ka + KV_WIDTH_A
    v_ref[0, 0:KV_WIDTH_A, :] = proj[:, o_va:o_va + KV_WIDTH_A].T.astype(BF16)

    o_cq = o_va + KV_WIDTH_A
    cq_t = proj[:, o_cq:o_cq + Q_LORA].T
    rq = _rsqrt_mean(jnp.sum(cq_t * cq_t, axis=0, keepdims=True), Q_LORA)
    cqn = (cq_t * rq) * gcq_ref[0]
    qb_t = jnp.dot(wuqt_ref[0], cqn.astype(BF16), preferred_element_type=F32)
    qscale_b = ((QK_NOPE_DIM + QK_ROPE_DIM) ** -0.5) * LOG2E
    zeros_pad = jnp.zeros((HEAD_SLOT - QK_NOPE_DIM - QK_ROPE_DIM, tm), F32)
    for j in range(N_HEADS_B):
        base = HEAD_SLOT * j
        nope = qb_t[base:base + QK_NOPE_DIM]
        rope = _rope_t(qb_t[base + ROPE_SLOT_OFF:base + ROPE_SLOT_OFF + QK_ROPE_DIM], cb, sb, QK_ROPE_DIM // 4)
        slot = jnp.concatenate([nope, rope, zeros_pad], axis=0) * qscale_b
        q_ref[0, N_HEADS_A * HEAD_SLOT + base:N_HEADS_A * HEAD_SLOT + base + HEAD_SLOT, :] = slot.astype(BF16)

    o_ckv = o_cq + Q_LORA
    ckv = proj[:, o_ckv:o_ckv + KV_LORA]
    rkv = _rsqrt_mean(jnp.sum(ckv * ckv, axis=-1, keepdims=True), KV_LORA)
    ckvn = (ckv * rkv) * gckv_ref[0]
    k_nope = jnp.dot(ckvn.astype(BF16), wuk_ref[0], preferred_element_type=F32)
    v_ref[0, KV_WIDTH_A:KV_WIDTH_A + WIDTH_B, :] = jnp.dot(
        wuvt_ref[0], ckvn.T.astype(BF16), preferred_element_type=F32).astype(BF16)
    o_kr = o_ckv + KV_LORA
    kr_t = proj[:, o_kr:o_kr + LANES].T
    kr_rope = _rope_t(kr_t[ROPE_SLOT_OFF:ROPE_SLOT_OFF + QK_ROPE_DIM], cb, sb, QK_ROPE_DIM // 4)
    kr_tile = jnp.concatenate(
        [jnp.zeros((ROPE_SLOT_OFF, tm), F32), kr_rope, zeros_pad], axis=0).T
    for j in range(N_HEADS_B):
        base = HEAD_SLOT * j
        k_ref[0, 1 + j] = (k_nope[:, base:base + HEAD_SLOT] + kr_tile).astype(BF16)


def _attn_in(x, mod, l, w, tabs, tm):
    nb, s, _ = x.shape
    grid = (nb, s // tm)
    lay = lambda *blk: pl.BlockSpec((1,) + blk, lambda b, i: (l,) + (0,) * len(blk))
    tab = lambda rows: pl.BlockSpec((rows, tm), lambda b, i: (0, i))
    out_shapes = (
        jax.ShapeDtypeStruct((nb, N_HEADS * HEAD_SLOT, s), BF16),
        jax.ShapeDtypeStruct((nb, N_KEY_SLABS, s, HEAD_SLOT), BF16),
        jax.ShapeDtypeStruct((nb, V_ROWS, s), BF16),
    )
    return pl.pallas_call(
        _attn_in_kernel,
        grid=grid,
        in_specs=[
            pl.BlockSpec((1, tm, D_MODEL), lambda b, i: (b, i, 0)),
            pl.BlockSpec((1, 1, N_MOD * D_MODEL), lambda b, i: (b, 0, 0)),
            lay(1, D_MODEL),
            lay(D_MODEL, D_IN_PAD),
            lay(HEAD_DIM_A, tm),
            lay(HEAD_DIM_A, tm),
            lay(Q_LORA, tm),
            lay(N_HEADS_B * HEAD_SLOT, Q_LORA),
            lay(1, KV_LORA),
            lay(KV_LORA, N_HEADS_B * HEAD_SLOT),
            lay(WIDTH_B, KV_LORA),
            tab(HEAD_DIM_A), tab(HEAD_DIM_A), tab(QK_ROPE_DIM), tab(QK_ROPE_DIM),
        ],
        out_specs=(pl.BlockSpec((1, N_HEADS * HEAD_SLOT, tm), lambda b, i: (b, 0, i)),
                   pl.BlockSpec((1, N_KEY_SLABS, tm, HEAD_SLOT), lambda b, i: (b, 0, i, 0)),
                   pl.BlockSpec((1, V_ROWS, tm), lambda b, i: (b, 0, i))),
        out_shape=out_shapes,
        compiler_params=pltpu.CompilerParams(
            dimension_semantics=("arbitrary", "arbitrary"), vmem_limit_bytes=VMEM_LIMIT),
        name="attn_in",
    )(x, mod, w["g_attn"], w["w_in"], w["gq_t"], w["gk_t"], w["gcq_t"], w["w_uq_t"], w["g_ckv"],
      w["w_uk"], w["w_uv_t"], tabs["ca"], tabs["sa"], tabs["cb"], tabs["sb"])


def _fold_rows(x, op):
    parts = [x[FOLD_ROWS * i:FOLD_ROWS * (i + 1)] for i in range(x.shape[0] // FOLD_ROWS)]
    while len(parts) > 1:
        parts = [op(parts[2 * i], parts[2 * i + 1]) for i in range(len(parts) // 2)]
    return parts[0]


def _key_slab(h):
    if isinstance(h, int):
        return 0 if h < N_HEADS_A else h - N_HEADS_A + 1
    return jnp.where(h < N_HEADS_A, 0, h - N_HEADS_A + 1)


def _value_row(h):
    per_kv = N_HEADS_A // N_KV_HEADS_A
    if isinstance(h, int):
        return HEAD_DV * (h // per_kv if h < N_HEADS_A else h - N_HEADS_A + N_KV_HEADS_A)
    return HEAD_DV * jnp.where(h < N_HEADS_A, h // per_kv, h - N_HEADS_A + N_KV_HEADS_A)


def _row_block(start, size):
    if isinstance(start, int):
        return pl.ds(start, size)
    return pl.ds(pl.multiple_of(start, size), size)


def _attention_kernel(q_ref, k_ref, v_ref, ot_ref, s0_ref, s1_ref):
    n_keys = k_ref.shape[2]
    tq = ot_ref.shape[2]
    n_chunks = n_keys // KEY_CHUNK
    rows = lambda c: slice(KEY_CHUNK * c, KEY_CHUNK * (c + 1))

    def stage(h_next, s_next_ref, h_cur, s_cur_ref, m_cur):
        m_next = None
        if h_next is not None:
            q_t = q_ref[0, _row_block(h_next * HEAD_SLOT, HEAD_SLOT), :]
            slab = _key_slab(h_next)
        if h_cur is not None:
            m_b = jnp.broadcast_to(jnp.max(m_cur, axis=0, keepdims=True), (KEY_CHUNK, tq))
            v_rows = _row_block(_value_row(h_cur), HEAD_DV)
            l_acc = jnp.zeros((FOLD_ROWS, tq), F32)
            o_acc = jnp.zeros((HEAD_DV, tq), F32)
        for c in range(n_chunks):
            if h_next is not None:
                s = jnp.dot(k_ref[0, slab, rows(c), :], q_t, preferred_element_type=F32)
                s_next_ref[rows(c), :] = s
                folded = _fold_rows(s, jnp.maximum)
                m_next = folded if m_next is None else jnp.maximum(m_next, folded)
            if h_cur is not None:
                p = jnp.exp2(s_cur_ref[rows(c), :] - m_b)
                l_acc = l_acc + _fold_rows(p, jnp.add)
                o_acc = o_acc + jnp.dot(v_ref[0, v_rows, rows(c)], p.astype(BF16),
                                        preferred_element_type=F32)
        if h_cur is not None:
            denom = jnp.sum(l_acc, axis=0, keepdims=True)
            ot_ref[0, _row_block(h_cur * HEAD_DV, HEAD_DV), :] = o_acc * (1.0 / denom)
        return m_next

    def pair(i, m_even):
        h = 2 * i
        m_odd = stage(h + 1, s1_ref, h, s0_ref, m_even)
        return stage(h + 2, s0_ref, h + 1, s1_ref, m_odd)

    assert N_HEADS % 2 == 0
    m_even = lax.fori_loop(0, N_HEADS // 2 - 1, pair, stage(0, s0_ref, None, None, None))
    m_odd = stage(N_HEADS - 1, s1_ref, N_HEADS - 2, s0_ref, m_even)
    stage(None, None, N_HEADS - 1, s1_ref, m_odd)


def _attention(q_all, k_all, v_all, tq):
    nb, _, s, _ = k_all.shape
    assert s % KEY_CHUNK == 0
    return pl.pallas_call(
        _attention_kernel,
        grid=(nb, s // tq),
        in_specs=[pl.BlockSpec((1, N_HEADS * HEAD_SLOT, tq), lambda b, i: (b, 0, i)),
                  pl.BlockSpec((1, N_KEY_SLABS, s, HEAD_SLOT), lambda b, i: (b, 0, 0, 0)),
                  pl.BlockSpec((1, V_ROWS, s), lambda b, i: (b, 0, 0))],
        out_specs=pl.BlockSpec((1, D_MIX, tq), lambda b, i: (b, 0, i)),
        out_shape=jax.ShapeDtypeStruct((nb, D_MIX, s), F32),
        scratch_shapes=[pltpu.VMEM((s, tq), F32), pltpu.VMEM((s, tq), F32)],
        compiler_params=pltpu.CompilerParams(
            dimension_semantics=("arbitrary", "arbitrary"), vmem_limit_bytes=VMEM_LIMIT),
        name="attention",
    )(q_all, k_all, v_all)


def _attn_out_kernel(ot_ref, x_ref, mod_ref, go_ref, wo_ref, gffn_ref, x1_ref, h2_ref):
    o_t = ot_ref[0]
    oa = o_t[:WIDTH_A]
    ob = o_t[WIDTH_A:]
    ra = _rsqrt_mean(jnp.sum(oa * oa, axis=0, keepdims=True), WIDTH_A)
    rb = _rsqrt_mean(jnp.sum(ob * ob, axis=0, keepdims=True), WIDTH_B)
    on = jnp.concatenate([oa * ra, ob * rb], axis=0).T * go_ref[0]
    res = jnp.dot(on.astype(BF16), wo_ref[0], preferred_element_type=F32)
    mod = mod_ref[0]
    gate_a = mod[:, 2 * D_MODEL:3 * D_MODEL]
    shift_f = mod[:, 3 * D_MODEL:4 * D_MODEL]
    scale_f = mod[:, 4 * D_MODEL:5 * D_MODEL]
    x1 = x_ref[0] + gate_a * res
    x1_ref[0] = x1
    r = _rsqrt_mean(jnp.sum(x1 * x1, axis=-1, keepdims=True), D_MODEL)
    h2_ref[0] = ((x1 * r) * gffn_ref[0] * (1.0 + scale_f) + shift_f).astype(BF16)


def _attn_out(o_t, x, mod, l, w, tm):
    nb, s, _ = x.shape
    lay = lambda *blk: pl.BlockSpec((1,) + blk, lambda b, i: (l,) + (0,) * len(blk))
    tok = pl.BlockSpec((1, tm, D_MODEL), lambda b, i: (b, i, 0))
    return pl.pallas_call(
        _attn_out_kernel,
        grid=(nb, s // tm),
        in_specs=[
            pl.BlockSpec((1, D_MIX, tm), lambda b, i: (b, 0, i)),
            tok,
            pl.BlockSpec((1, 1, N_MOD * D_MODEL), lambda b, i: (b, 0, 0)),
            lay(1, D_MIX),
            lay(D_MIX, D_MODEL),
            lay(1, D_MODEL),
        ],
        out_specs=(tok, tok),
        out_shape=(jax.ShapeDtypeStruct((nb, s, D_MODEL), F32), jax.ShapeDtypeStruct((nb, s, D_MODEL), BF16)),
        compiler_params=pltpu.CompilerParams(
            dimension_semantics=("arbitrary", "arbitrary"), vmem_limit_bytes=VMEM_LIMIT),
        name="attn_out",
    )(o_t, x, mod, w["g_out"], w["w_o"], w["g_ffn"])


def _ffn_kernel(hm_ref, hp_ref, hn_ref, x1_ref, mod_ref, wup_ref, cw_ref, cb_ref, wdn_ref, gfin_ref,
                out_ref, hext_ref, ua_ref, ub_ref, *, final):
    i = pl.program_id(1)
    n_i = pl.num_programs(1)
    tm = hm_ref.shape[1]

    keep_prev = jnp.where(i > 0, 1.0, 0.0).astype(F32)
    keep_next = jnp.where(i < n_i - 1, 1.0, 0.0).astype(F32)
    hext_ref[0:HALO, :] = (hp_ref[0, 0].astype(F32) * keep_prev).astype(BF16)
    hext_ref[HALO:HALO + tm, :] = hm_ref[0]
    hext_ref[HALO + tm:HALO + tm + HALO, :] = (hn_ref[0, 0].astype(F32) * keep_next).astype(BF16)
    out_ref[0] = jnp.zeros((tm, D_MODEL), F32)

    def up(c, u_ref):
        u = jnp.dot(hext_ref[...], wup_ref[0, c], preferred_element_type=F32)
        for j in range(FF_SLABS):
            u_ref[j] = u[:, LANES * j:LANES * (j + 1)]

    def mix(c, u_ref):
        cw = cw_ref[0, c]
        cb = cb_ref[0, c]
        ys = []
        for j in range(FF_SLABS):
            lanes = slice(LANES * j, LANES * (j + 1))
            ys.append(u_ref[j, HALO - 1:HALO - 1 + tm, :] * cw[0:1, lanes]
                      + u_ref[j, HALO:HALO + tm, :] * cw[1:2, lanes]
                      + u_ref[j, HALO + 1:HALO + 1 + tm, :] * cw[2:3, lanes] + cb[:, lanes])
        half = FF_SLABS // 2
        acts = []
        for j in range(half):
            gate = ys[half + j]
            acts.append((gate * (1.0 / (1.0 + jnp.exp(-gate)))) * ys[j])
        act = jnp.concatenate(acts, axis=1).astype(BF16)
        out_ref[0] += jnp.dot(act, wdn_ref[0, c], preferred_element_type=F32)

    def pair(k, carry):
        c = 2 * k
        up(c + 1, ub_ref)
        mix(c, ua_ref)
        up(c + 2, ua_ref)
        mix(c + 1, ub_ref)
        return carry

    assert N_FF_CHUNKS % 2 == 1
    up(0, ua_ref)
    lax.fori_loop(0, N_FF_CHUNKS // 2, pair, 0)
    mix(N_FF_CHUNKS - 1, ua_ref)

    gate_f = mod_ref[0][:, 5 * D_MODEL:6 * D_MODEL]
    x2 = x1_ref[0] + gate_f * out_ref[0]
    if final:
        r = _rsqrt_mean(jnp.sum(x2 * x2, axis=-1, keepdims=True), D_MODEL)
        x2 = (x2 * r) * gfin_ref[...]
    out_ref[0] = x2


def _ffn(h2, x1, mod, l, w, g_final, tm, final):
    nb, s, _ = x1.shape
    n_i = s // tm
    hb = tm // HALO
    h2_blocks = h2.reshape(nb, s // HALO, HALO, D_MODEL)
    tok = pl.BlockSpec((1, tm, D_MODEL), lambda b, i: (b, i, 0))
    resident = lambda *blk: pl.BlockSpec((1,) + blk, lambda b, i: (l,) + (0,) * len(blk),
                                         pipeline_mode=pl.Buffered(1))
    u_buf = pltpu.VMEM((FF_SLABS, tm + 2 * HALO, LANES), F32)
    return pl.pallas_call(
        functools.partial(_ffn_kernel, final=final),
        grid=(nb, n_i),
        in_specs=[
            tok,
            pl.BlockSpec((1, 1, HALO, D_MODEL), lambda b, i: (b, jnp.maximum(i * hb - 1, 0), 0, 0)),
            pl.BlockSpec((1, 1, HALO, D_MODEL),
                         lambda b, i: (b, jnp.minimum((i + 1) * hb, s // HALO - 1), 0, 0)),
            tok,
            pl.BlockSpec((1, 1, N_MOD * D_MODEL), lambda b, i: (b, 0, 0)),
            resident(N_FF_CHUNKS, D_MODEL, 2 * FF_CHUNK),
            resident(N_FF_CHUNKS, 3, 2 * FF_CHUNK),
            resident(N_FF_CHUNKS, 1, 2 * FF_CHUNK),
            resident(N_FF_CHUNKS, FF_CHUNK, D_MODEL),
            pl.BlockSpec((1, D_MODEL), lambda b, i: (0, 0)),
        ],
        out_specs=tok,
        out_shape=jax.ShapeDtypeStruct((nb, s, D_MODEL), F32),
        scratch_shapes=[pltpu.VMEM((tm + 2 * HALO, D_MODEL), BF16), u_buf, u_buf],
        compiler_params=pltpu.CompilerParams(
            dimension_semantics=("arbitrary", "arbitrary"), vmem_limit_bytes=VMEM_LIMIT),
        name="ffn",
    )(h2, h2_blocks, h2_blocks, x1, mod, w["w_up"], w["conv_w"], w["conv_b"], w["w_down"], g_final)


def _rope_tables_t(n_tokens, rot_dim):
    rows = n_tokens // GRID_W
    row = jnp.repeat(jnp.arange(rows, dtype=F32), GRID_W)
    col = jnp.tile(jnp.arange(GRID_W, dtype=F32), rows)
    quarter = rot_dim // 4
    inv_freq = ROPE_THETA ** (-jnp.arange(quarter, dtype=F32) / quarter)
    ang = jnp.stack([row, col], axis=-1)[:, :, None] * inv_freq
    cos, sin = jnp.cos(ang), jnp.sin(ang)
    c = jnp.concatenate([cos[:, 0], cos[:, 0], cos[:, 1], cos[:, 1]], axis=-1).T
    s = jnp.concatenate([-sin[:, 0], sin[:, 0], -sin[:, 1], sin[:, 1]], axis=-1).T
    return c, s


def _prepare_weights(g_attn, w_in, g_q_a, g_k_a, g_cq, w_uq, g_ckv, w_ukv, g_out_a, g_out_b, w_o,
                     g_ffn, w_up, conv_w, conv_b, w_down, tm):
    o_kr = D_IN - QK_ROPE_DIM
    w_in_p = jnp.concatenate([
        w_in[..., :o_kr], jnp.zeros((DEPTH, D_MODEL, ROPE_SLOT_OFF), F32), w_in[..., o_kr:],
        jnp.zeros((DEPTH, D_MODEL, LANES - ROPE_SLOT_OFF - QK_ROPE_DIM), F32)], axis=-1).astype(BF16)
    hq = QK_NOPE_DIM + QK_ROPE_DIM
    w_uq_p = jnp.pad(w_uq.reshape(DEPTH, Q_LORA, N_HEADS_B, hq), ((0, 0), (0, 0), (0, 0), (0, HEAD_SLOT - hq)))
    w_uq_t = w_uq_p.reshape(DEPTH, Q_LORA, N_HEADS_B * HEAD_SLOT).transpose(0, 2, 1).astype(BF16)
    w_ukv_h = w_ukv.reshape(DEPTH, KV_LORA, N_HEADS_B, QK_NOPE_DIM + V_DIM_B)
    w_uk = jnp.pad(w_ukv_h[..., :QK_NOPE_DIM], ((0, 0), (0, 0), (0, 0), (0, HEAD_SLOT - QK_NOPE_DIM)))
    w_uk = w_uk.reshape(DEPTH, KV_LORA, N_HEADS_B * HEAD_SLOT).astype(BF16)
    w_uv_t = w_ukv_h[..., QK_NOPE_DIM:].reshape(DEPTH, KV_LORA, WIDTH_B).transpose(0, 2, 1).astype(BF16)
    halves = lambda a: jnp.stack([a[..., :D_FF].reshape(a.shape[:-1] + (N_FF_CHUNKS, FF_CHUNK)),
                                  a[..., D_FF:].reshape(a.shape[:-1] + (N_FF_CHUNKS, FF_CHUNK))], axis=-2)
    w_up_c = halves(w_up).transpose(0, 2, 1, 3, 4).reshape(DEPTH, N_FF_CHUNKS, D_MODEL, 2 * FF_CHUNK).astype(BF16)
    conv_w_c = halves(conv_w).transpose(0, 2, 1, 3, 4).reshape(DEPTH, N_FF_CHUNKS, 3, 2 * FF_CHUNK)
    conv_b_c = halves(conv_b).reshape(DEPTH, N_FF_CHUNKS, 1, 2 * FF_CHUNK)
    w_down_c = w_down.reshape(DEPTH, N_FF_CHUNKS, FF_CHUNK, D_MODEL).astype(BF16)
    col = lambda g: jnp.broadcast_to(g[:, :, None], g.shape + (tm,))
    return {
        "g_attn": g_attn.reshape(DEPTH, 1, D_MODEL), "w_in": w_in_p,
        "gq_t": col(g_q_a), "gk_t": col(g_k_a), "gcq_t": col(g_cq), "w_uq_t": w_uq_t,
        "g_ckv": g_ckv.reshape(DEPTH, 1, KV_LORA), "w_uk": w_uk, "w_uv_t": w_uv_t,
        "g_out": jnp.concatenate([g_out_a, g_out_b], axis=-1).reshape(DEPTH, 1, D_MIX),
        "w_o": w_o.astype(BF16), "g_ffn": g_ffn.reshape(DEPTH, 1, D_MODEL),
        "w_up": w_up_c, "conv_w": conv_w_c, "conv_b": conv_b_c, "w_down": w_down_c,
    }


def _tiles(s):
    return min(512, s), min(512, s), min(512, s)


def kernel(x_prompt, x_sample, c_prompt, c_sample, w_ada, b_ada, g_attn, w_in, g_q_a, g_k_a, g_cq, w_uq,
           g_ckv, w_ukv, g_out_a, g_out_b, w_o, g_ffn, w_up, conv_w, conv_b, w_down, g_final):
    n_prompt = x_prompt.shape[0]
    s = x_prompt.shape[1]
    assert x_sample.shape[1] == s and s % GRID_W == 0
    tm, tq, tf = _tiles(s)
    assert s % tm == 0 and s % tq == 0 and s % tf == 0 and tf % HALO == 0
    x = jnp.concatenate([x_prompt, x_sample], axis=0)
    c = jnp.concatenate([c_prompt, c_sample], axis=0)
    nb = x.shape[0]

    w = _prepare_weights(g_attn, w_in, g_q_a, g_k_a, g_cq, w_uq, g_ckv, w_ukv, g_out_a, g_out_b, w_o,
                         g_ffn, w_up, conv_w, conv_b, w_down, tm)
    ca, sa = _rope_tables_t(s, HEAD_DIM_A)
    cb, sb = _rope_tables_t(s, QK_ROPE_DIM)
    tabs = {"ca": ca, "sa": sa, "cb": cb, "sb": sb}
    mod_all = _modulation(c, w_ada, b_ada)
    g_fin = g_final.reshape(1, D_MODEL)

    for l in range(DEPTH):
        mod = mod_all[l].reshape(nb, 1, N_MOD * D_MODEL)
        q_all, k_all, v_all = _attn_in(x, mod, l, w, tabs, tm)
        o_t = _attention(q_all, k_all, v_all, tq)
        x1, h2 = _attn_out(o_t, x, mod, l, w, tm)
        x = _ffn(h2, x1, mod, l, w, g_fin, tf, final=(l == DEPTH - 1))
    return x[:n_prompt], x[n_prompt:]
```

```python
import functools
import math

import jax
import jax.numpy as jnp
from jax import lax
from jax.experimental import pallas as pl
from jax.experimental.pallas import tpu as pltpu

F32 = jnp.float32
BF16 = jnp.bfloat16

D_MODEL = 1024
DEPTH = 4
GRID_W = 64
ROPE_THETA = 10000.0
EPS = 1e-6
N_HEADS_A = 8
N_KV_HEADS_A = 2
HEAD_DIM_A = 64
N_HEADS_B = 8
QK_NOPE_DIM = 64
QK_ROPE_DIM = 32
V_DIM_B = 64
Q_LORA = 256
KV_LORA = 128
WIDTH_A = N_HEADS_A * HEAD_DIM_A
WIDTH_B = N_HEADS_B * V_DIM_B
D_MIX = WIDTH_A + WIDTH_B
KV_WIDTH_A = N_KV_HEADS_A * HEAD_DIM_A
D_IN = WIDTH_A + 2 * KV_WIDTH_A + Q_LORA + KV_LORA + QK_ROPE_DIM
D_FF = 2816
N_MOD = 6

LANES = 128
HEAD_SLOT = 128
D_IN_PAD = 1280
ROPE_SLOT_OFF = 64
FF_CHUNK = 256
N_FF_CHUNKS = D_FF // FF_CHUNK
FF_SLABS = 2 * FF_CHUNK // LANES
HALO = 16
N_HEADS = N_HEADS_A + N_HEADS_B
N_KEY_SLABS = 1 + N_HEADS_B
HEAD_DV = HEAD_DIM_A
assert HEAD_DV == V_DIM_B
V_SLOT = HEAD_DV + 16
V_ROWS = (N_KV_HEADS_A + N_HEADS_B) * V_SLOT
KEY_CHUNK = 256
FOLD_ROWS = 8
LOG2E = math.log2(math.e)
VMEM_LIMIT = 56 * 1024 * 1024


def _rsqrt_mean(ss, n):
    return lax.rsqrt(ss * (1.0 / n) + EPS)


def _mod_kernel(c_ref, w_ref, b_ref, o_ref):
    c = c_ref[...]
    ca = c * (1.0 / (1.0 + jnp.exp(-c)))
    o_ref[0] = jnp.dot(ca.astype(BF16), w_ref[0].astype(BF16), preferred_element_type=F32) + b_ref[0]


def _modulation(c, w_ada, b_ada):
    nb = c.shape[0]
    n_col = N_MOD * D_MODEL
    tn = 1024
    return pl.pallas_call(
        _mod_kernel,
        grid=(DEPTH, n_col // tn),
        in_specs=[
            pl.BlockSpec((nb, D_MODEL), lambda l, j: (0, 0)),
            pl.BlockSpec((1, D_MODEL, tn), lambda l, j: (l, 0, j)),
            pl.BlockSpec((1, 1, tn), lambda l, j: (l, 0, j)),
        ],
        out_specs=pl.BlockSpec((1, nb, tn), lambda l, j: (l, 0, j)),
        out_shape=jax.ShapeDtypeStruct((DEPTH, nb, n_col), F32),
        compiler_params=pltpu.CompilerParams(dimension_semantics=("arbitrary", "arbitrary")),
        name="modulation",
    )(c, w_ada, b_ada.reshape(DEPTH, 1, n_col))


def _rope_t(x, c, s, q):
    sw = jnp.concatenate([x[q:2 * q], x[0:q], x[3 * q:4 * q], x[2 * q:3 * q]], axis=0)
    return x * c + sw * s


def _token_specs(xs, tm, n_i):
    if len(xs) == 1:
        return [pl.BlockSpec((1, tm, D_MODEL), lambda b, i: (b, i, 0))]
    n_p = xs[0].shape[0]
    return [pl.BlockSpec((1, tm, D_MODEL),
                         lambda b, i: (jnp.minimum(b, n_p - 1), jnp.where(b < n_p, i, n_i - 1), 0)),
            pl.BlockSpec((1, tm, D_MODEL),
                         lambda b, i: (jnp.maximum(b - n_p, 0), jnp.where(b >= n_p, i, 0), 0))]


def _load_tokens(x_refs, n_prompt):
    if len(x_refs) == 1:
        return x_refs[0][0]
    return jnp.where(pl.program_id(0) < n_prompt, x_refs[0][0], x_refs[1][0])


def _attn_in_kernel(*refs, n_x, n_prompt):
    x_refs = refs[:n_x]
    (mod_ref, gat_ref, win_ref, gq_ref, gk_ref, gcq_ref, wuqt_ref, gckv_ref, wuk_ref, wuvt_ref,
     ca_ref, sa_ref, cb_ref, sb_ref, q_ref, k_ref, v_ref) = refs[n_x:]
    x = _load_tokens(x_refs, n_prompt)
    tm = x.shape[0]
    mod = mod_ref[0]
    shift = mod[:, 0:D_MODEL]
    scale = mod[:, D_MODEL:2 * D_MODEL]
    r = _rsqrt_mean(jnp.sum(x * x, axis=-1, keepdims=True), D_MODEL)
    h = (x * r) * gat_ref[0] * (1.0 + scale) + shift
    proj = jnp.dot(h.astype(BF16), win_ref[0], preferred_element_type=F32)

    ca = ca_ref[...]
    sa = sa_ref[...]
    cb = cb_ref[...]
    sb = sb_ref[...]
    zeros_half = jnp.zeros((HEAD_DIM_A, tm), F32)

    o_qa = 0
    q_at = proj[:, o_qa:o_qa + WIDTH_A].T
    qscale_a = (HEAD_DIM_A ** -0.5) * LOG2E
    gq = gq_ref[0]
    for j in range(N_HEADS_A):
        xh = q_at[HEAD_DIM_A * j:HEAD_DIM_A * (j + 1)]
        rh = _rsqrt_mean(jnp.sum(xh * xh, axis=0, keepdims=True), HEAD_DIM_A)
        xr = _rope_t((xh * rh) * gq, ca, sa, HEAD_DIM_A // 4) * qscale_a
        g = j // (N_HEADS_A // N_KV_HEADS_A)
        slot = jnp.concatenate([xr, zeros_half] if g == 0 else [zeros_half, xr], axis=0)
        q_ref[0, HEAD_SLOT * j:HEAD_SLOT * (j + 1), :] = slot.astype(BF16)

    o_ka = WIDTH_A
    k_at = proj[:, o_ka:o_ka + KV_WIDTH_A].T
    gk = gk_ref[0]
    kparts = []
    for g in range(N_KV_HEADS_A):
        xh = k_at[HEAD_DIM_A * g:HEAD_DIM_A * (g + 1)]
        rh = _rsqrt_mean(jnp.sum(xh * xh, axis=0, keepdims=True), HEAD_DIM_A)
        kparts.append(_rope_t((xh * rh) * gk, ca, sa, HEAD_DIM_A // 4))
    k_ref[0, 0] = jnp.concatenate(kparts, axis=0).T.astype(BF16)
    o_va = o_ka + KV_WIDTH_A
    ones_rows = jnp.where(lax.broadcasted_iota(jnp.int32, (V_SLOT - HEAD_DV, tm), 0) == 0, 1.0, 0.0).astype(BF16)

    def store_values(slot, v_t):
        v_ref[0, V_SLOT * slot:V_SLOT * slot + HEAD_DV, :] = v_t.astype(BF16)
        v_ref[0, V_SLOT * slot + HEAD_DV:V_SLOT * (slot + 1), :] = ones_rows

    va_t = proj[:, o_va:o_va + KV_WIDTH_A].T
    for g in range(N_KV_HEADS_A):
        store_values(g, va_t[HEAD_DV * g:HEAD_DV * (g + 1)])

    o_cq = o_va + KV_WIDTH_A
    cq_t = proj[:, o_cq:o_cq + Q_LORA].T
    rq = _rsqrt_mean(jnp.sum(cq_t * cq_t, axis=0, keepdims=True), Q_LORA)
    cqn = (cq_t * rq) * gcq_ref[0]
    qb_t = jnp.dot(wuqt_ref[0], cqn.astype(BF16), preferred_element_type=F32)
    qscale_b = ((QK_NOPE_DIM + QK_ROPE_DIM) ** -0.5) * LOG2E
    zeros_pad = jnp.zeros((HEAD_SLOT - QK_NOPE_DIM - QK_ROPE_DIM, tm), F32)
    for j in range(N_HEADS_B):
        base = HEAD_SLOT * j
        nope = qb_t[base:base + QK_NOPE_DIM]
        rope = _rope_t(qb_t[base + ROPE_SLOT_OFF:base + ROPE_SLOT_OFF + QK_ROPE_DIM], cb, sb, QK_ROPE_DIM // 4)
        slot = jnp.concatenate([nope, rope, zeros_pad], axis=0) * qscale_b
        q_ref[0, N_HEADS_A * HEAD_SLOT + base:N_HEADS_A * HEAD_SLOT + base + HEAD_SLOT, :] = slot.astype(BF16)

    o_ckv = o_cq + Q_LORA
    ckv = proj[:, o_ckv:o_ckv + KV_LORA]
    rkv = _rsqrt_mean(jnp.sum(ckv * ckv, axis=-1, keepdims=True), KV_LORA)
    ckvn = (ckv * rkv) * gckv_ref[0]
    k_nope = jnp.dot(ckvn.astype(BF16), wuk_ref[0], preferred_element_type=F32)
    vb_t = jnp.dot(wuvt_ref[0], ckvn.T.astype(BF16), preferred_element_type=F32)
    for j in range(N_HEADS_B):
        store_values(N_KV_HEADS_A + j, vb_t[HEAD_DV * j:HEAD_DV * (j + 1)])
    o_kr = o_ckv + KV_LORA
    kr_t = proj[:, o_kr:o_kr + LANES].T
    kr_rope = _rope_t(kr_t[ROPE_SLOT_OFF:ROPE_SLOT_OFF + QK_ROPE_DIM], cb, sb, QK_ROPE_DIM // 4)
    kr_tile = jnp.concatenate(
        [jnp.zeros((ROPE_SLOT_OFF, tm), F32), kr_rope, zeros_pad], axis=0).T
    for j in range(N_HEADS_B):
        base = HEAD_SLOT * j
        k_ref[0, 1 + j] = (k_nope[:, base:base + HEAD_SLOT] + kr_tile).astype(BF16)


def _attn_in(xs, mod, l, w, tabs, tm):
    nb = sum(x.shape[0] for x in xs)
    s = xs[0].shape[1]
    grid = (nb, s // tm)
    lay = lambda *blk: pl.BlockSpec((1,) + blk, lambda b, i: (l,) + (0,) * len(blk))
    tab = lambda rows: pl.BlockSpec((rows, tm), lambda b, i: (0, i))
    out_shapes = (
        jax.ShapeDtypeStruct((nb, N_HEADS * HEAD_SLOT, s), BF16),
        jax.ShapeDtypeStruct((nb, N_KEY_SLABS, s, HEAD_SLOT), BF16),
        jax.ShapeDtypeStruct((nb, V_ROWS, s), BF16),
    )
    return pl.pallas_call(
        functools.partial(_attn_in_kernel, n_x=len(xs), n_prompt=xs[0].shape[0]),
        grid=grid,
        in_specs=_token_specs(xs, tm, s // tm) + [
            pl.BlockSpec((1, 1, N_MOD * D_MODEL), lambda b, i: (b, 0, 0)),
            lay(1, D_MODEL),
            lay(D_MODEL, D_IN_PAD),
            lay(HEAD_DIM_A, tm),
            lay(HEAD_DIM_A, tm),
            lay(Q_LORA, tm),
            lay(N_HEADS_B * HEAD_SLOT, Q_LORA),
            lay(1, KV_LORA),
            lay(KV_LORA, N_HEADS_B * HEAD_SLOT),
            lay(WIDTH_B, KV_LORA),
            tab(HEAD_DIM_A), tab(HEAD_DIM_A), tab(QK_ROPE_DIM), tab(QK_ROPE_DIM),
        ],
        out_specs=(pl.BlockSpec((1, N_HEADS * HEAD_SLOT, tm), lambda b, i: (b, 0, i)),
                   pl.BlockSpec((1, N_KEY_SLABS, tm, HEAD_SLOT), lambda b, i: (b, 0, i, 0)),
                   pl.BlockSpec((1, V_ROWS, tm), lambda b, i: (b, 0, i))),
        out_shape=out_shapes,
        compiler_params=pltpu.CompilerParams(
            dimension_semantics=("arbitrary", "arbitrary"), vmem_limit_bytes=VMEM_LIMIT),
        name="attn_in",
    )(*xs, mod, w["g_attn"], w["w_in"], w["gq_t"], w["gk_t"], w["gcq_t"], w["w_uq_t"], w["g_ckv"],
      w["w_uk"], w["w_uv_t"], tabs["ca"], tabs["sa"], tabs["cb"], tabs["sb"])


def _fold_rows(x, op):
    parts = [x[FOLD_ROWS * i:FOLD_ROWS * (i + 1)] for i in range(x.shape[0] // FOLD_ROWS)]
    while len(parts) > 1:
        parts = [op(parts[2 * i], parts[2 * i + 1]) for i in range(len(parts) // 2)]
    return parts[0]


def _key_slab(h):
    if isinstance(h, int):
        return 0 if h < N_HEADS_A else h - N_HEADS_A + 1
    return jnp.where(h < N_HEADS_A, 0, h - N_HEADS_A + 1)


def _value_row(h):
    per_kv = N_HEADS_A // N_KV_HEADS_A
    if isinstance(h, int):
        return V_SLOT * (h // per_kv if h < N_HEADS_A else h - N_HEADS_A + N_KV_HEADS_A)
    return V_SLOT * jnp.where(h < N_HEADS_A, h // per_kv, h - N_HEADS_A + N_KV_HEADS_A)


def _row_block(start, size):
    if isinstance(start, int):
        return pl.ds(start, size)
    return pl.ds(pl.multiple_of(start, size), size)


def _attention_kernel(q_ref, k_ref, v_ref, ot_ref, s0_ref, s1_ref):
    n_keys = k_ref.shape[2]
    w = ot_ref.shape[2] // 2
    n_chunks = n_keys // KEY_CHUNK
    rows = lambda c: slice(KEY_CHUNK * c, KEY_CHUNK * (c + 1))
    cols = lambda t: slice(w * t, w * (t + 1))

    def stage(nxt, s_next_ref, cur, s_cur_ref, m_cur):
        m_next = None
        if nxt is not None:
            h_next, t_next = nxt
            q_t = q_ref[0, _row_block(h_next * HEAD_SLOT, HEAD_SLOT), cols(t_next)]
            slab = _key_slab(h_next)
        if cur is not None:
            h_cur, t_cur = cur
            m_b = jnp.broadcast_to(jnp.max(m_cur, axis=0, keepdims=True), (KEY_CHUNK, w))
            v_rows = _row_block(_value_row(h_cur), V_SLOT)
            o_acc = jnp.zeros((V_SLOT, w), F32)
        for c in range(n_chunks):
            if nxt is not None:
                s = jnp.dot(k_ref[0, slab, rows(c), :], q_t, preferred_element_type=F32)
                s_next_ref[rows(c), :] = s
                folded = _fold_rows(s, jnp.maximum)
                m_next = folded if m_next is None else jnp.maximum(m_next, folded)
            if cur is not None:
                p = jnp.exp2(s_cur_ref[rows(c), :] - m_b)
                o_acc = o_acc + jnp.dot(v_ref[0, v_rows, rows(c)], p.astype(BF16),
                                        preferred_element_type=F32)
        if cur is not None:
            denom = o_acc[HEAD_DV:HEAD_DV + 1]
            ot_ref[0, _row_block(h_cur * HEAD_DV, HEAD_DV), cols(t_cur)] = o_acc[:HEAD_DV] * (1.0 / denom)
        return m_next

    def head(h, m0):
        m1 = stage((h, 1), s1_ref, (h, 0), s0_ref, m0)
        return stage((h + 1, 0), s0_ref, (h, 1), s1_ref, m1)

    m0 = lax.fori_loop(0, N_HEADS - 1, head, stage((0, 0), s0_ref, None, None, None))
    m1 = stage((N_HEADS - 1, 1), s1_ref, (N_HEADS - 1, 0), s0_ref, m0)
    stage(None, None, (N_HEADS - 1, 1), s1_ref, m1)


def _attention(q_all, k_all, v_all, tq):
    nb, _, s, _ = k_all.shape
    assert s % KEY_CHUNK == 0 and tq % (2 * LANES) == 0
    score_buf = pltpu.VMEM((s, tq // 2), F32)
    return pl.pallas_call(
        _attention_kernel,
        grid=(nb, s // tq),
        in_specs=[pl.BlockSpec((1, N_HEADS * HEAD_SLOT, tq), lambda b, i: (b, 0, i)),
                  pl.BlockSpec((1, N_KEY_SLABS, s, HEAD_SLOT), lambda b, i: (b, 0, 0, 0)),
                  pl.BlockSpec((1, V_ROWS, s), lambda b, i: (b, 0, 0))],
        out_specs=pl.BlockSpec((1, D_MIX, tq), lambda b, i: (b, 0, i)),
        out_shape=jax.ShapeDtypeStruct((nb, D_MIX, s), F32),
        scratch_shapes=[score_buf, score_buf],
        compiler_params=pltpu.CompilerParams(
            dimension_semantics=("arbitrary", "arbitrary"), vmem_limit_bytes=VMEM_LIMIT),
        name="attention",
    )(q_all, k_all, v_all)


def _attn_out_kernel(ot_ref, *refs, n_x, n_prompt):
    x_refs = refs[:n_x]
    mod_ref, go_ref, wo_ref, gffn_ref, x1_ref, h2_ref = refs[n_x:]
    o_t = ot_ref[0]
    oa = o_t[:WIDTH_A]
    ob = o_t[WIDTH_A:]
    ra = _rsqrt_mean(jnp.sum(oa * oa, axis=0, keepdims=True), WIDTH_A)
    rb = _rsqrt_mean(jnp.sum(ob * ob, axis=0, keepdims=True), WIDTH_B)
    on = jnp.concatenate([oa * ra, ob * rb], axis=0).T * go_ref[0]
    res = jnp.dot(on.astype(BF16), wo_ref[0], preferred_element_type=F32)
    mod = mod_ref[0]
    gate_a = mod[:, 2 * D_MODEL:3 * D_MODEL]
    shift_f = mod[:, 3 * D_MODEL:4 * D_MODEL]
    scale_f = mod[:, 4 * D_MODEL:5 * D_MODEL]
    x1 = _load_tokens(x_refs, n_prompt) + gate_a * res
    x1_ref[0] = x1
    r = _rsqrt_mean(jnp.sum(x1 * x1, axis=-1, keepdims=True), D_MODEL)
    h2_ref[0] = ((x1 * r) * gffn_ref[0] * (1.0 + scale_f) + shift_f).astype(BF16)


def _attn_out(o_t, xs, mod, l, w, tm):
    nb, _, s = o_t.shape
    lay = lambda *blk: pl.BlockSpec((1,) + blk, lambda b, i: (l,) + (0,) * len(blk))
    tok = pl.BlockSpec((1, tm, D_MODEL), lambda b, i: (b, i, 0))
    return pl.pallas_call(
        functools.partial(_attn_out_kernel, n_x=len(xs), n_prompt=xs[0].shape[0]),
        grid=(nb, s // tm),
        in_specs=[pl.BlockSpec((1, D_MIX, tm), lambda b, i: (b, 0, i))] + _token_specs(xs, tm, s // tm) + [
            pl.BlockSpec((1, 1, N_MOD * D_MODEL), lambda b, i: (b, 0, 0)),
            lay(1, D_MIX),
            lay(D_MIX, D_MODEL),
            lay(1, D_MODEL),
        ],
        out_specs=(tok, tok),
        out_shape=(jax.ShapeDtypeStruct((nb, s, D_MODEL), F32), jax.ShapeDtypeStruct((nb, s, D_MODEL), BF16)),
        compiler_params=pltpu.CompilerParams(
            dimension_semantics=("arbitrary", "arbitrary"), vmem_limit_bytes=VMEM_LIMIT),
        name="attn_out",
    )(o_t, *xs, mod, w["g_out"], w["w_o"], w["g_ffn"])


def _ffn_kernel(hm_ref, hp_ref, hn_ref, x1_ref, mod_ref, wup_ref, cw_ref, cb_ref, wdn_ref, gfin_ref,
                *refs, final, n_prompt):
    out_refs = refs[:-4]
    hext_ref, ua_ref, ub_ref, acc_ref = refs[-4:]
    i = pl.program_id(1)
    n_i = pl.num_programs(1)
    tm = hm_ref.shape[1]

    keep_prev = jnp.where(i > 0, 1.0, 0.0).astype(F32)
    keep_next = jnp.where(i < n_i - 1, 1.0, 0.0).astype(F32)
    hext_ref[0:HALO, :] = (hp_ref[0].astype(F32) * keep_prev).astype(BF16)
    hext_ref[HALO:HALO + tm, :] = hm_ref[0]
    hext_ref[HALO + tm:HALO + tm + HALO, :] = (hn_ref[0].astype(F32) * keep_next).astype(BF16)
    acc_ref[...] = jnp.zeros((tm, D_MODEL), F32)

    def up(c, u_ref):
        u = jnp.dot(hext_ref[...], wup_ref[0, c], preferred_element_type=F32)
        for j in range(FF_SLABS):
            u_ref[j] = u[:, LANES * j:LANES * (j + 1)]

    def mix(c, u_ref):
        cw = cw_ref[0, c]
        cb = cb_ref[0, c]
        ys = []
        for j in range(FF_SLABS):
            lanes = slice(LANES * j, LANES * (j + 1))
            ys.append(u_ref[j, HALO - 1:HALO - 1 + tm, :] * cw[0:1, lanes]
                      + u_ref[j, HALO:HALO + tm, :] * cw[1:2, lanes]
                      + u_ref[j, HALO + 1:HALO + 1 + tm, :] * cw[2:3, lanes] + cb[:, lanes])
        half = FF_SLABS // 2
        acts = []
        for j in range(half):
            gate = ys[half + j]
            acts.append((gate * (1.0 / (1.0 + jnp.exp(-gate)))) * ys[j])
        act = jnp.concatenate(acts, axis=1).astype(BF16)
        acc_ref[...] += jnp.dot(act, wdn_ref[0, c], preferred_element_type=F32)

    def pair(k, carry):
        c = 2 * k
        up(c + 1, ub_ref)
        mix(c, ua_ref)
        up(c + 2, ua_ref)
        mix(c + 1, ub_ref)
        return carry

    assert N_FF_CHUNKS % 2 == 1
    up(0, ua_ref)
    lax.fori_loop(0, N_FF_CHUNKS // 2, pair, 0)
    mix(N_FF_CHUNKS - 1, ua_ref)

    gate_f = mod_ref[0][:, 5 * D_MODEL:6 * D_MODEL]
    x2 = x1_ref[0] + gate_f * acc_ref[...]
    if final:
        r = _rsqrt_mean(jnp.sum(x2 * x2, axis=-1, keepdims=True), D_MODEL)
        x2 = (x2 * r) * gfin_ref[...]
    if len(out_refs) == 1:
        out_refs[0][0] = x2
    else:
        is_prompt = pl.program_id(0) < n_prompt

        @pl.when(is_prompt)
        def _():
            out_refs[0][0] = x2

        @pl.when(jnp.logical_not(is_prompt))
        def _():
            out_refs[1][0] = x2


def _ffn(h2, x1, mod, l, w, g_final, tm, n_prompt, final):
    nb, s, _ = x1.shape
    n_i = s // tm
    if final:
        out_shape = (jax.ShapeDtypeStruct((n_prompt, s, D_MODEL), F32),
                     jax.ShapeDtypeStruct((nb - n_prompt, s, D_MODEL), F32))
    else:
        out_shape = (jax.ShapeDtypeStruct((nb, s, D_MODEL), F32),)
    hb = tm // HALO
    tok = pl.BlockSpec((1, tm, D_MODEL), lambda b, i: (b, i, 0))
    resident = lambda *blk: pl.BlockSpec((1,) + blk, lambda b, i: (l,) + (0,) * len(blk),
                                         pipeline_mode=pl.Buffered(1))
    u_buf = pltpu.VMEM((FF_SLABS, tm + 2 * HALO, LANES), F32)
    return pl.pallas_call(
        functools.partial(_ffn_kernel, final=final, n_prompt=n_prompt),
        grid=(nb, n_i),
        in_specs=[
            tok,
            pl.BlockSpec((1, HALO, D_MODEL), lambda b, i: (b, jnp.maximum(i * hb - 1, 0), 0)),
            pl.BlockSpec((1, HALO, D_MODEL), lambda b, i: (b, jnp.minimum((i + 1) * hb, s // HALO - 1), 0)),
            tok,
            pl.BlockSpec((1, 1, N_MOD * D_MODEL), lambda b, i: (b, 0, 0)),
            resident(N_FF_CHUNKS, D_MODEL, 2 * FF_CHUNK),
            resident(N_FF_CHUNKS, 3, 2 * FF_CHUNK),
            resident(N_FF_CHUNKS, 1, 2 * FF_CHUNK),
            resident(N_FF_CHUNKS, FF_CHUNK, D_MODEL),
            pl.BlockSpec((1, D_MODEL), lambda b, i: (0, 0)),
        ],
        out_specs=tuple(_token_specs(out_shape, tm, n_i)),
        out_shape=out_shape,
        scratch_shapes=[pltpu.VMEM((tm + 2 * HALO, D_MODEL), BF16), u_buf, u_buf,
                        pltpu.VMEM((tm, D_MODEL), F32)],
        compiler_params=pltpu.CompilerParams(
            dimension_semantics=("arbitrary", "arbitrary"), vmem_limit_bytes=VMEM_LIMIT),
        name="ffn",
    )(h2, h2, h2, x1, mod, w["w_up"], w["conv_w"], w["conv_b"], w["w_down"], g_final)


def _rope_tables_t(n_tokens, rot_dim):
    rows = n_tokens // GRID_W
    row = jnp.repeat(jnp.arange(rows, dtype=F32), GRID_W)
    col = jnp.tile(jnp.arange(GRID_W, dtype=F32), rows)
    quarter = rot_dim // 4
    inv_freq = ROPE_THETA ** (-jnp.arange(quarter, dtype=F32) / quarter)
    ang = jnp.stack([row, col], axis=-1)[:, :, None] * inv_freq
    cos, sin = jnp.cos(ang), jnp.sin(ang)
    c = jnp.concatenate([cos[:, 0], cos[:, 0], cos[:, 1], cos[:, 1]], axis=-1).T
    s = jnp.concatenate([-sin[:, 0], sin[:, 0], -sin[:, 1], sin[:, 1]], axis=-1).T
    return c, s


def _prepare_weights(g_attn, w_in, g_q_a, g_k_a, g_cq, w_uq, g_ckv, w_ukv, g_out_a, g_out_b, w_o,
                     g_ffn, w_up, conv_w, conv_b, w_down, tm):
    o_kr = D_IN - QK_ROPE_DIM
    w_in_p = jnp.concatenate([
        w_in[..., :o_kr], jnp.zeros((DEPTH, D_MODEL, ROPE_SLOT_OFF), F32), w_in[..., o_kr:],
        jnp.zeros((DEPTH, D_MODEL, LANES - ROPE_SLOT_OFF - QK_ROPE_DIM), F32)], axis=-1).astype(BF16)
    hq = QK_NOPE_DIM + QK_ROPE_DIM
    w_uq_p = jnp.pad(w_uq.reshape(DEPTH, Q_LORA, N_HEADS_B, hq), ((0, 0), (0, 0), (0, 0), (0, HEAD_SLOT - hq)))
    w_uq_t = w_uq_p.reshape(DEPTH, Q_LORA, N_HEADS_B * HEAD_SLOT).transpose(0, 2, 1).astype(BF16)
    w_ukv_h = w_ukv.reshape(DEPTH, KV_LORA, N_HEADS_B, QK_NOPE_DIM + V_DIM_B)
    w_uk = jnp.pad(w_ukv_h[..., :QK_NOPE_DIM], ((0, 0), (0, 0), (0, 0), (0, HEAD_SLOT - QK_NOPE_DIM)))
    w_uk = w_uk.reshape(DEPTH, KV_LORA, N_HEADS_B * HEAD_SLOT).astype(BF16)
    w_uv_t = w_ukv_h[..., QK_NOPE_DIM:].reshape(DEPTH, KV_LORA, WIDTH_B).transpose(0, 2, 1).astype(BF16)
    halves = lambda a: jnp.stack([a[..., :D_FF].reshape(a.shape[:-1] + (N_FF_CHUNKS, FF_CHUNK)),
                                  a[..., D_FF:].reshape(a.shape[:-1] + (N_FF_CHUNKS, FF_CHUNK))], axis=-2)
    w_up_c = halves(w_up).transpose(0, 2, 1, 3, 4).reshape(DEPTH, N_FF_CHUNKS, D_MODEL, 2 * FF_CHUNK).astype(BF16)
    conv_w_c = halves(conv_w).transpose(0, 2, 1, 3, 4).reshape(DEPTH, N_FF_CHUNKS, 3, 2 * FF_CHUNK)
    conv_b_c = halves(conv_b).reshape(DEPTH, N_FF_CHUNKS, 1, 2 * FF_CHUNK)
    w_down_c = w_down.reshape(DEPTH, N_FF_CHUNKS, FF_CHUNK, D_MODEL).astype(BF16)
    col = lambda g: jnp.broadcast_to(g[:, :, None], g.shape + (tm,))
    return {
        "g_attn": g_attn.reshape(DEPTH, 1, D_MODEL), "w_in": w_in_p,
        "gq_t": col(g_q_a), "gk_t": col(g_k_a), "gcq_t": col(g_cq), "w_uq_t": w_uq_t,
        "g_ckv": g_ckv.reshape(DEPTH, 1, KV_LORA), "w_uk": w_uk, "w_uv_t": w_uv_t,
        "g_out": jnp.concatenate([g_out_a, g_out_b], axis=-1).reshape(DEPTH, 1, D_MIX),
        "w_o": w_o.astype(BF16), "g_ffn": g_ffn.reshape(DEPTH, 1, D_MODEL),
        "w_up": w_up_c, "conv_w": conv_w_c, "conv_b": conv_b_c, "w_down": w_down_c,
    }


def _tiles(s):
    return min(512, s), min(1024, s), min(512, s)


def kernel(x_prompt, x_sample, c_prompt, c_sample, w_ada, b_ada, g_attn, w_in, g_q_a, g_k_a, g_cq, w_uq,
           g_ckv, w_ukv, g_out_a, g_out_b, w_o, g_ffn, w_up, conv_w, conv_b, w_down, g_final):
    n_prompt = x_prompt.shape[0]
    s = x_prompt.shape[1]
    assert x_sample.shape[1] == s and s % GRID_W == 0
    tm, tq, tf = _tiles(s)
    assert s % tm == 0 and s % tq == 0 and s % tf == 0 and tf % HALO == 0
    xs = (x_prompt, x_sample)
    c = jnp.concatenate([c_prompt, c_sample], axis=0)
    nb = c.shape[0]

    w = _prepare_weights(g_attn, w_in, g_q_a, g_k_a, g_cq, w_uq, g_ckv, w_ukv, g_out_a, g_out_b, w_o,
                         g_ffn, w_up, conv_w, conv_b, w_down, tm)
    ca, sa = _rope_tables_t(s, HEAD_DIM_A)
    cb, sb = _rope_tables_t(s, QK_ROPE_DIM)
    tabs = {"ca": ca, "sa": sa, "cb": cb, "sb": sb}
    mod_all = _modulation(c, w_ada, b_ada)
    g_fin = g_final.reshape(1, D_MODEL)

    for l in range(DEPTH):
        mod = mod_all[l].reshape(nb, 1, N_MOD * D_MODEL)
        q_all, k_all, v_all = _attn_in(xs, mod, l, w, tabs, tm)
        o_t = _attention(q_all, k_all, v_all, tq)
        x1, h2 = _attn_out(o_t, xs, mod, l, w, tm)
        xs = _ffn(h2, x1, mod, l, w, g_fin, tf, n_prompt, final=(l == DEPTH - 1))
    return tuple(xs)
```

```python
import functools
import math

import jax
import jax.numpy as jnp
from jax import lax
from jax.experimental import pallas as pl
from jax.experimental.pallas import tpu as pltpu

F32 = jnp.float32
BF16 = jnp.bfloat16

D_MODEL = 1024
DEPTH = 4
GRID_W = 64
ROPE_THETA = 10000.0
EPS = 1e-6
N_HEADS_A = 8
N_KV_HEADS_A = 2
HEAD_DIM_A = 64
N_HEADS_B = 8
QK_NOPE_DIM = 64
QK_ROPE_DIM = 32
V_DIM_B = 64
Q_LORA = 256
KV_LORA = 128
WIDTH_A = N_HEADS_A * HEAD_DIM_A
WIDTH_B = N_HEADS_B * V_DIM_B
D_MIX = WIDTH_A + WIDTH_B
KV_WIDTH_A = N_KV_HEADS_A * HEAD_DIM_A
D_IN = WIDTH_A + 2 * KV_WIDTH_A + Q_LORA + KV_LORA + QK_ROPE_DIM
D_FF = 2816
N_MOD = 6

LANES = 128
HEAD_SLOT = 128
D_IN_PAD = 1280
ROPE_SLOT_OFF = 64
FF_CHUNK = 256
N_FF_CHUNKS = D_FF // FF_CHUNK
FF_SLABS = 2 * FF_CHUNK // LANES
HALO = 16
N_HEADS = N_HEADS_A + N_HEADS_B
N_KEY_SLABS = 1 + N_HEADS_B
HEAD_DV = HEAD_DIM_A
assert HEAD_DV == V_DIM_B
V_SLOT = HEAD_DV + 16
V_ROWS = (N_KV_HEADS_A + N_HEADS_B) * V_SLOT
KEY_CHUNK = 256
FOLD_ROWS = 8
SUB_ROWS = 256
LOG2E = math.log2(math.e)
VMEM_LIMIT = 56 * 1024 * 1024


def _rsqrt_mean(ss, n):
    return lax.rsqrt(ss * (1.0 / n) + EPS)


def _mod_kernel(c_ref, w_ref, b_ref, o_ref):
    c = c_ref[...]
    ca = c * (1.0 / (1.0 + jnp.exp(-c)))
    o_ref[0] = jnp.dot(ca.astype(BF16), w_ref[0].astype(BF16), preferred_element_type=F32) + b_ref[0]


def _modulation(c, w_ada, b_ada):
    nb = c.shape[0]
    n_col = N_MOD * D_MODEL
    tn = 1024
    return pl.pallas_call(
        _mod_kernel,
        grid=(DEPTH, n_col // tn),
        in_specs=[
            pl.BlockSpec((nb, D_MODEL), lambda l, j: (0, 0)),
            pl.BlockSpec((1, D_MODEL, tn), lambda l, j: (l, 0, j)),
            pl.BlockSpec((1, 1, tn), lambda l, j: (l, 0, j)),
        ],
        out_specs=pl.BlockSpec((1, nb, tn), lambda l, j: (l, 0, j)),
        out_shape=jax.ShapeDtypeStruct((DEPTH, nb, n_col), F32),
        compiler_params=pltpu.CompilerParams(dimension_semantics=("arbitrary", "arbitrary")),
        name="modulation",
    )(c, w_ada, b_ada.reshape(DEPTH, 1, n_col))


def _rope_t(x, c, s, q):
    sw = jnp.concatenate([x[q:2 * q], x[0:q], x[3 * q:4 * q], x[2 * q:3 * q]], axis=0)
    return x * c + sw * s


def _token_specs(xs, tm, n_i):
    if len(xs) == 1:
        return [pl.BlockSpec((1, tm, D_MODEL), lambda b, i: (b, i, 0))]
    n_p = xs[0].shape[0]
    return [pl.BlockSpec((1, tm, D_MODEL),
                         lambda b, i: (jnp.minimum(b, n_p - 1), jnp.where(b < n_p, i, n_i - 1), 0)),
            pl.BlockSpec((1, tm, D_MODEL),
                         lambda b, i: (jnp.maximum(b - n_p, 0), jnp.where(b >= n_p, i, 0), 0))]


def _load_tokens(x_refs, n_prompt):
    if len(x_refs) == 1:
        return x_refs[0][0]
    return jnp.where(pl.program_id(0) < n_prompt, x_refs[0][0], x_refs[1][0])


def _attn_in_kernel(*refs, n_x, n_prompt):
    x_refs = refs[:n_x]
    (mod_ref, gat_ref, win_ref, gq_ref, gk_ref, gcq_ref, wuqt_ref, gckv_ref, wuk_ref, wuvt_ref,
     ca_ref, sa_ref, cb_ref, sb_ref, q_ref, k_ref, v_ref) = refs[n_x:]
    x_all = _load_tokens(x_refs, n_prompt)
    tm_all = x_all.shape[0]
    tm = min(SUB_ROWS, tm_all)
    mod = mod_ref[0]
    shift = mod[:, 0:D_MODEL]
    scale = mod[:, D_MODEL:2 * D_MODEL]

    def project(t):
        x = x_all[tm * t:tm * (t + 1)]
        r = _rsqrt_mean(jnp.sum(x * x, axis=-1, keepdims=True), D_MODEL)
        h = (x * r) * gat_ref[0] * (1.0 + scale) + shift
        return jnp.dot(h.astype(BF16), win_ref[0], preferred_element_type=F32)

    def finish(t, proj):
        tok = slice(tm * t, tm * (t + 1))
        ca = ca_ref[:, tok]
        sa = sa_ref[:, tok]
        cb = cb_ref[:, tok]
        sb = sb_ref[:, tok]
        zeros_half = jnp.zeros((HEAD_DIM_A, tm), F32)

        o_qa = 0
        q_at = proj[:, o_qa:o_qa + WIDTH_A].T
        qscale_a = (HEAD_DIM_A ** -0.5) * LOG2E
        gq = gq_ref[0, :, tok]
        for j in range(N_HEADS_A):
            xh = q_at[HEAD_DIM_A * j:HEAD_DIM_A * (j + 1)]
            rh = _rsqrt_mean(jnp.sum(xh * xh, axis=0, keepdims=True), HEAD_DIM_A)
            xr = _rope_t((xh * rh) * gq, ca, sa, HEAD_DIM_A // 4) * qscale_a
            g = j // (N_HEADS_A // N_KV_HEADS_A)
            slot = jnp.concatenate([xr, zeros_half] if g == 0 else [zeros_half, xr], axis=0)
            q_ref[0, HEAD_SLOT * j:HEAD_SLOT * (j + 1), tok] = slot.astype(BF16)

        o_ka = WIDTH_A
        k_at = proj[:, o_ka:o_ka + KV_WIDTH_A].T
        gk = gk_ref[0, :, tok]
        kparts = []
        for g in range(N_KV_HEADS_A):
            xh = k_at[HEAD_DIM_A * g:HEAD_DIM_A * (g + 1)]
            rh = _rsqrt_mean(jnp.sum(xh * xh, axis=0, keepdims=True), HEAD_DIM_A)
            kparts.append(_rope_t((xh * rh) * gk, ca, sa, HEAD_DIM_A // 4))
        k_ref[0, 0, tok, :] = jnp.concatenate(kparts, axis=0).T.astype(BF16)
        o_va = o_ka + KV_WIDTH_A
        ones_rows = jnp.where(
            lax.broadcasted_iota(jnp.int32, (V_SLOT - HEAD_DV, tm), 0) == 0, 1.0, 0.0).astype(BF16)

        def store_values(slot, v_t):
            v_ref[0, V_SLOT * slot:V_SLOT * slot + HEAD_DV, tok] = v_t.astype(BF16)
            v_ref[0, V_SLOT * slot + HEAD_DV:V_SLOT * (slot + 1), tok] = ones_rows

        va_t = proj[:, o_va:o_va + KV_WIDTH_A].T
        for g in range(N_KV_HEADS_A):
            store_values(g, va_t[HEAD_DV * g:HEAD_DV * (g + 1)])

        o_cq = o_va + KV_WIDTH_A
        cq_t = proj[:, o_cq:o_cq + Q_LORA].T
        rq = _rsqrt_mean(jnp.sum(cq_t * cq_t, axis=0, keepdims=True), Q_LORA)
        cqn = (cq_t * rq) * gcq_ref[0, :, tok]
        qb_t = jnp.dot(wuqt_ref[0], cqn.astype(BF16), preferred_element_type=F32)
        qscale_b = ((QK_NOPE_DIM + QK_ROPE_DIM) ** -0.5) * LOG2E
        zeros_pad = jnp.zeros((HEAD_SLOT - QK_NOPE_DIM - QK_ROPE_DIM, tm), F32)
        for j in range(N_HEADS_B):
            base = HEAD_SLOT * j
            nope = qb_t[base:base + QK_NOPE_DIM]
            rope = _rope_t(qb_t[base + ROPE_SLOT_OFF:base + ROPE_SLOT_OFF + QK_ROPE_DIM], cb, sb,
                           QK_ROPE_DIM // 4)
            slot = jnp.concatenate([nope, rope, zeros_pad], axis=0) * qscale_b
            row0 = N_HEADS_A * HEAD_SLOT + base
            q_ref[0, row0:row0 + HEAD_SLOT, tok] = slot.astype(BF16)

        o_ckv = o_cq + Q_LORA
        ckv = proj[:, o_ckv:o_ckv + KV_LORA]
        rkv = _rsqrt_mean(jnp.sum(ckv * ckv, axis=-1, keepdims=True), KV_LORA)
        ckvn = (ckv * rkv) * gckv_ref[0]
        k_nope = jnp.dot(ckvn.astype(BF16), wuk_ref[0], preferred_element_type=F32)
        vb_t = jnp.dot(wuvt_ref[0], ckvn.T.astype(BF16), preferred_element_type=F32)
        for j in range(N_HEADS_B):
            store_values(N_KV_HEADS_A + j, vb_t[HEAD_DV * j:HEAD_DV * (j + 1)])
        o_kr = o_ckv + KV_LORA
        kr_t = proj[:, o_kr:o_kr + LANES].T
        kr_rope = _rope_t(kr_t[ROPE_SLOT_OFF:ROPE_SLOT_OFF + QK_ROPE_DIM], cb, sb, QK_ROPE_DIM // 4)
        kr_tile = jnp.concatenate(
            [jnp.zeros((ROPE_SLOT_OFF, tm), F32), kr_rope, zeros_pad], axis=0).T
        for j in range(N_HEADS_B):
            base = HEAD_SLOT * j
            k_ref[0, 1 + j, tok, :] = (k_nope[:, base:base + HEAD_SLOT] + kr_tile).astype(BF16)

    n_sub = tm_all // tm
    projs = [project(t) for t in range(n_sub)]
    for t in range(n_sub):
        finish(t, projs[t])


def _attn_in(xs, mod, l, w, tabs, tm):
    nb = sum(x.shape[0] for x in xs)
    s = xs[0].shape[1]
    grid = (nb, s // tm)
    lay = lambda *blk: pl.BlockSpec((1,) + blk, lambda b, i: (l,) + (0,) * len(blk))
    tab = lambda rows: pl.BlockSpec((rows, tm), lambda b, i: (0, i))
    out_shapes = (
        jax.ShapeDtypeStruct((nb, N_HEADS * HEAD_SLOT, s), BF16),
        jax.ShapeDtypeStruct((nb, N_KEY_SLABS, s, HEAD_SLOT), BF16),
        jax.ShapeDtypeStruct((nb, V_ROWS, s), BF16),
    )
    return pl.pallas_call(
        functools.partial(_attn_in_kernel, n_x=len(xs), n_prompt=xs[0].shape[0]),
        grid=grid,
        in_specs=_token_specs(xs, tm, s // tm) + [
            pl.BlockSpec((1, 1, N_MOD * D_MODEL), lambda b, i: (b, 0, 0)),
            lay(1, D_MODEL),
            lay(D_MODEL, D_IN_PAD),
            lay(HEAD_DIM_A, tm),
            lay(HEAD_DIM_A, tm),
            lay(Q_LORA, tm),
            lay(N_HEADS_B * HEAD_SLOT, Q_LORA),
            lay(1, KV_LORA),
            lay(KV_LORA, N_HEADS_B * HEAD_SLOT),
            lay(WIDTH_B, KV_LORA),
            tab(HEAD_DIM_A), tab(HEAD_DIM_A), tab(QK_ROPE_DIM), tab(QK_ROPE_DIM),
        ],
        out_specs=(pl.BlockSpec((1, N_HEADS * HEAD_SLOT, tm), lambda b, i: (b, 0, i)),
                   pl.BlockSpec((1, N_KEY_SLABS, tm, HEAD_SLOT), lambda b, i: (b, 0, i, 0)),
                   pl.BlockSpec((1, V_ROWS, tm), lambda b, i: (b, 0, i))),
        out_shape=out_shapes,
        compiler_params=pltpu.CompilerParams(
            dimension_semantics=("arbitrary", "arbitrary"), vmem_limit_bytes=VMEM_LIMIT),
        name="attn_in",
    )(*xs, mod, w["g_attn"], w["w_in"], w["gq_t"], w["gk_t"], w["gcq_t"], w["w_uq_t"], w["g_ckv"],
      w["w_uk"], w["w_uv_t"], tabs["ca"], tabs["sa"], tabs["cb"], tabs["sb"])


def _fold_rows(x, op):
    parts = [x[FOLD_ROWS * i:FOLD_ROWS * (i + 1)] for i in range(x.shape[0] // FOLD_ROWS)]
    while len(parts) > 1:
        parts = [op(parts[2 * i], parts[2 * i + 1]) for i in range(len(parts) // 2)]
    return parts[0]


def _key_slab(h):
    if isinstance(h, int):
        return 0 if h < N_HEADS_A else h - N_HEADS_A + 1
    return jnp.where(h < N_HEADS_A, 0, h - N_HEADS_A + 1)


def _value_row(h):
    per_kv = N_HEADS_A // N_KV_HEADS_A
    if isinstance(h, int):
        return V_SLOT * (h // per_kv if h < N_HEADS_A else h - N_HEADS_A + N_KV_HEADS_A)
    return V_SLOT * jnp.where(h < N_HEADS_A, h // per_kv, h - N_HEADS_A + N_KV_HEADS_A)


def _row_block(start, size):
    if isinstance(start, int):
        return pl.ds(start, size)
    return pl.ds(pl.multiple_of(start, size), size)


def _attention_kernel(q_ref, k_ref, v_ref, ot_ref, s0_ref, s1_ref):
    n_keys = k_ref.shape[2]
    w = ot_ref.shape[2] // 2
    n_chunks = n_keys // KEY_CHUNK
    rows = lambda c: slice(KEY_CHUNK * c, KEY_CHUNK * (c + 1))
    cols = lambda t: slice(w * t, w * (t + 1))

    def stage(nxt, s_next_ref, cur, s_cur_ref, m_cur):
        m_next = None
        if nxt is not None:
            h_next, t_next = nxt
            q_t = q_ref[0, _row_block(h_next * HEAD_SLOT, HEAD_SLOT), cols(t_next)]
            slab = _key_slab(h_next)
        if cur is not None:
            h_cur, t_cur = cur
            m_b = jnp.broadcast_to(jnp.max(m_cur, axis=0, keepdims=True), (KEY_CHUNK, w))
            v_rows = _row_block(_value_row(h_cur), V_SLOT)
            o_acc = jnp.zeros((V_SLOT, w), F32)
        for c in range(n_chunks):
            if nxt is not None:
                s = jnp.dot(k_ref[0, slab, rows(c), :], q_t, preferred_element_type=F32)
                s_next_ref[rows(c), :] = s
                folded = _fold_rows(s, jnp.maximum)
                m_next = folded if m_next is None else jnp.maximum(m_next, folded)
            if cur is not None:
                p = jnp.exp2(s_cur_ref[rows(c), :] - m_b)
                o_acc = o_acc + jnp.dot(v_ref[0, v_rows, rows(c)], p.astype(BF16),
                                        preferred_element_type=F32)
        if cur is not None:
            denom = o_acc[HEAD_DV:HEAD_DV + 1]
            ot_ref[0, _row_block(h_cur * HEAD_DV, HEAD_DV), cols(t_cur)] = (
                o_acc[:HEAD_DV] * (1.0 / denom)).astype(ot_ref.dtype)
        return m_next

    def head(h, m0):
        m1 = stage((h, 1), s1_ref, (h, 0), s0_ref, m0)
        return stage((h + 1, 0), s0_ref, (h, 1), s1_ref, m1)

    m0 = lax.fori_loop(0, N_HEADS - 1, head, stage((0, 0), s0_ref, None, None, None))
    m1 = stage((N_HEADS - 1, 1), s1_ref, (N_HEADS - 1, 0), s0_ref, m0)
    stage(None, None, (N_HEADS - 1, 1), s1_ref, m1)


def _attention(q_all, k_all, v_all, tq):
    nb, _, s, _ = k_all.shape
    assert s % KEY_CHUNK == 0 and tq % (2 * LANES) == 0
    score_buf = pltpu.VMEM((s, tq // 2), F32)
    return pl.pallas_call(
        _attention_kernel,
        grid=(nb, s // tq),
        in_specs=[pl.BlockSpec((1, N_HEADS * HEAD_SLOT, tq), lambda b, i: (b, 0, i)),
                  pl.BlockSpec((1, N_KEY_SLABS, s, HEAD_SLOT), lambda b, i: (b, 0, 0, 0)),
                  pl.BlockSpec((1, V_ROWS, s), lambda b, i: (b, 0, 0))],
        out_specs=pl.BlockSpec((1, D_MIX, tq), lambda b, i: (b, 0, i)),
        out_shape=jax.ShapeDtypeStruct((nb, D_MIX, s), BF16),
        scratch_shapes=[score_buf, score_buf],
        compiler_params=pltpu.CompilerParams(
            dimension_semantics=("arbitrary", "arbitrary"), vmem_limit_bytes=VMEM_LIMIT),
        name="attention",
    )(q_all, k_all, v_all)


def _attn_out_kernel(ot_ref, *refs, n_x, n_prompt):
    x_refs = refs[:n_x]
    mod_ref, go_ref, wo_ref, gffn_ref, x1_ref, h2_ref = refs[n_x:]
    o_t = ot_ref[0].astype(F32)
    oa = o_t[:WIDTH_A]
    ob = o_t[WIDTH_A:]
    ra = _rsqrt_mean(jnp.sum(oa * oa, axis=0, keepdims=True), WIDTH_A)
    rb = _rsqrt_mean(jnp.sum(ob * ob, axis=0, keepdims=True), WIDTH_B)
    on = jnp.concatenate([oa * ra, ob * rb], axis=0).T * go_ref[0]
    res = jnp.dot(on.astype(BF16), wo_ref[0], preferred_element_type=F32)
    mod = mod_ref[0]
    gate_a = mod[:, 2 * D_MODEL:3 * D_MODEL]
    shift_f = mod[:, 3 * D_MODEL:4 * D_MODEL]
    scale_f = mod[:, 4 * D_MODEL:5 * D_MODEL]
    x1 = _load_tokens(x_refs, n_prompt) + gate_a * res
    x1_ref[0] = x1
    r = _rsqrt_mean(jnp.sum(x1 * x1, axis=-1, keepdims=True), D_MODEL)
    h2_ref[0] = ((x1 * r) * gffn_ref[0] * (1.0 + scale_f) + shift_f).astype(BF16)


def _attn_out(o_t, xs, mod, l, w, tm):
    nb, _, s = o_t.shape
    lay = lambda *blk: pl.BlockSpec((1,) + blk, lambda b, i: (l,) + (0,) * len(blk))
    tok = pl.BlockSpec((1, tm, D_MODEL), lambda b, i: (b, i, 0))
    return pl.pallas_call(
        functools.partial(_attn_out_kernel, n_x=len(xs), n_prompt=xs[0].shape[0]),
        grid=(nb, s // tm),
        in_specs=[pl.BlockSpec((1, D_MIX, tm), lambda b, i: (b, 0, i))] + _token_specs(xs, tm, s // tm) + [
            pl.BlockSpec((1, 1, N_MOD * D_MODEL), lambda b, i: (b, 0, 0)),
            lay(1, D_MIX),
            lay(D_MIX, D_MODEL),
            lay(1, D_MODEL),
        ],
        out_specs=(tok, tok),
        out_shape=(jax.ShapeDtypeStruct((nb, s, D_MODEL), F32), jax.ShapeDtypeStruct((nb, s, D_MODEL), BF16)),
        compiler_params=pltpu.CompilerParams(
            dimension_semantics=("arbitrary", "arbitrary"), vmem_limit_bytes=VMEM_LIMIT),
        name="attn_out",
    )(o_t, *xs, mod, w["g_out"], w["w_o"], w["g_ffn"])


def _ffn_kernel(hm_ref, hp_ref, hn_ref, x1_ref, mod_ref, wup_ref, cw_ref, cb_ref, wdn_ref, gfin_ref,
                *refs, final, n_prompt):
    n_out = 2 if final else 1
    out_refs = refs[:n_out]
    hext_ref, ua_ref, ub_ref = refs[n_out:n_out + 3]
    acc_ref = refs[n_out + 3] if final else out_refs[0].at[0]
    i = pl.program_id(1)
    n_i = pl.num_programs(1)
    tm = hm_ref.shape[1]

    keep_prev = jnp.where(i > 0, 1.0, 0.0).astype(F32)
    keep_next = jnp.where(i < n_i - 1, 1.0, 0.0).astype(F32)
    hext_ref[0:HALO, :] = (hp_ref[0].astype(F32) * keep_prev).astype(BF16)
    hext_ref[HALO:HALO + tm, :] = hm_ref[0]
    hext_ref[HALO + tm:HALO + tm + HALO, :] = (hn_ref[0].astype(F32) * keep_next).astype(BF16)
    acc_ref[...] = jnp.zeros((tm, D_MODEL), F32)

    def up(c, u_ref):
        u = jnp.dot(hext_ref[...], wup_ref[0, c], preferred_element_type=F32)
        for j in range(FF_SLABS):
            u_ref[j] = u[:, LANES * j:LANES * (j + 1)]

    def mix(c, u_ref):
        cw = cw_ref[0, c]
        cb = cb_ref[0, c]
        ys = []
        for j in range(FF_SLABS):
            lanes = slice(LANES * j, LANES * (j + 1))
            ys.append(u_ref[j, HALO - 1:HALO - 1 + tm, :] * cw[0:1, lanes]
                      + u_ref[j, HALO:HALO + tm, :] * cw[1:2, lanes]
                      + u_ref[j, HALO + 1:HALO + 1 + tm, :] * cw[2:3, lanes] + cb[:, lanes])
        half = FF_SLABS // 2
        acts = []
        for j in range(half):
            gate = ys[half + j]
            acts.append((gate * (1.0 / (1.0 + jnp.exp(-gate)))) * ys[j])
        act = jnp.concatenate(acts, axis=1).astype(BF16)
        acc_ref[...] += jnp.dot(act, wdn_ref[0, c], preferred_element_type=F32)

    def pair(k, carry):
        c = 2 * k
        up(c + 1, ub_ref)
        mix(c, ua_ref)
        up(c + 2, ua_ref)
        mix(c + 1, ub_ref)
        return carry

    assert N_FF_CHUNKS % 2 == 1
    up(0, ua_ref)
    lax.fori_loop(0, N_FF_CHUNKS // 2, pair, 0)
    mix(N_FF_CHUNKS - 1, ua_ref)

    gate_f = mod_ref[0][:, 5 * D_MODEL:6 * D_MODEL]
    x2 = x1_ref[0] + gate_f * acc_ref[...]
    if final:
        r = _rsqrt_mean(jnp.sum(x2 * x2, axis=-1, keepdims=True), D_MODEL)
        x2 = (x2 * r) * gfin_ref[...]
    if not final:
        out_refs[0][0] = x2
    else:
        is_prompt = pl.program_id(0) < n_prompt

        @pl.when(is_prompt)
        def _():
            out_refs[0][0] = x2

        @pl.when(jnp.logical_not(is_prompt))
        def _():
            out_refs[1][0] = x2


def _ffn(h2, x1, mod, l, w, g_final, tm, n_prompt, final):
    nb, s, _ = x1.shape
    n_i = s // tm
    if final:
        out_shape = (jax.ShapeDtypeStruct((n_prompt, s, D_MODEL), F32),
                     jax.ShapeDtypeStruct((nb - n_prompt, s, D_MODEL), F32))
    else:
        out_shape = (jax.ShapeDtypeStruct((nb, s, D_MODEL), F32),)
    hb = tm // HALO
    tok = pl.BlockSpec((1, tm, D_MODEL), lambda b, i: (b, i, 0))
    resident = lambda *blk: pl.BlockSpec((1,) + blk, lambda b, i: (l,) + (0,) * len(blk),
                                         pipeline_mode=pl.Buffered(1))
    u_buf = pltpu.VMEM((FF_SLABS, tm + 2 * HALO, LANES), F32)
    return pl.pallas_call(
        functools.partial(_ffn_kernel, final=final, n_prompt=n_prompt),
        grid=(nb, n_i),
        in_specs=[
            tok,
            pl.BlockSpec((1, HALO, D_MODEL), lambda b, i: (b, jnp.maximum(i * hb - 1, 0), 0)),
            pl.BlockSpec((1, HALO, D_MODEL), lambda b, i: (b, jnp.minimum((i + 1) * hb, s // HALO - 1), 0)),
            tok,
            pl.BlockSpec((1, 1, N_MOD * D_MODEL), lambda b, i: (b, 0, 0)),
            resident(N_FF_CHUNKS, D_MODEL, 2 * FF_CHUNK),
            resident(N_FF_CHUNKS, 3, 2 * FF_CHUNK),
            resident(N_FF_CHUNKS, 1, 2 * FF_CHUNK),
            resident(N_FF_CHUNKS, FF_CHUNK, D_MODEL),
            pl.BlockSpec((1, D_MODEL), lambda b, i: (0, 0)),
        ],
        out_specs=tuple(_token_specs(out_shape, tm, n_i)),
        out_shape=out_shape,
        scratch_shapes=[pltpu.VMEM((tm + 2 * HALO, D_MODEL), BF16), u_buf, u_buf]
        + ([pltpu.VMEM((tm, D_MODEL), F32)] if final else []),
        compiler_params=pltpu.CompilerParams(
            dimension_semantics=("arbitrary", "arbitrary"), vmem_limit_bytes=VMEM_LIMIT),
        name="ffn",
    )(h2, h2, h2, x1, mod, w["w_up"], w["conv_w"], w["conv_b"], w["w_down"], g_final)


def _rope_tables_t(n_tokens, rot_dim):
    rows = n_tokens // GRID_W
    row = jnp.repeat(jnp.arange(rows, dtype=F32), GRID_W)
    col = jnp.tile(jnp.arange(GRID_W, dtype=F32), rows)
    quarter = rot_dim // 4
    inv_freq = ROPE_THETA ** (-jnp.arange(quarter, dtype=F32) / quarter)
    ang = jnp.stack([row, col], axis=-1)[:, :, None] * inv_freq
    cos, sin = jnp.cos(ang), jnp.sin(ang)
    c = jnp.concatenate([cos[:, 0], cos[:, 0], cos[:, 1], cos[:, 1]], axis=-1).T
    s = jnp.concatenate([-sin[:, 0], sin[:, 0], -sin[:, 1], sin[:, 1]], axis=-1).T
    return c, s


def _prepare_weights(g_attn, w_in, g_q_a, g_k_a, g_cq, w_uq, g_ckv, w_ukv, g_out_a, g_out_b, w_o,
                     g_ffn, w_up, conv_w, conv_b, w_down, tm):
    o_kr = D_IN - QK_ROPE_DIM
    w_in_p = jnp.concatenate([
        w_in[..., :o_kr], jnp.zeros((DEPTH, D_MODEL, ROPE_SLOT_OFF), F32), w_in[..., o_kr:],
        jnp.zeros((DEPTH, D_MODEL, LANES - ROPE_SLOT_OFF - QK_ROPE_DIM), F32)], axis=-1).astype(BF16)
    hq = QK_NOPE_DIM + QK_ROPE_DIM
    w_uq_p = jnp.pad(w_uq.reshape(DEPTH, Q_LORA, N_HEADS_B, hq), ((0, 0), (0, 0), (0, 0), (0, HEAD_SLOT - hq)))
    w_uq_t = w_uq_p.reshape(DEPTH, Q_LORA, N_HEADS_B * HEAD_SLOT).transpose(0, 2, 1).astype(BF16)
    w_ukv_h = w_ukv.reshape(DEPTH, KV_LORA, N_HEADS_B, QK_NOPE_DIM + V_DIM_B)
    w_uk = jnp.pad(w_ukv_h[..., :QK_NOPE_DIM], ((0, 0), (0, 0), (0, 0), (0, HEAD_SLOT - QK_NOPE_DIM)))
    w_uk = w_uk.reshape(DEPTH, KV_LORA, N_HEADS_B * HEAD_SLOT).astype(BF16)
    w_uv_t = w_ukv_h[..., QK_NOPE_DIM:].reshape(DEPTH, KV_LORA, WIDTH_B).transpose(0, 2, 1).astype(BF16)
    halves = lambda a: jnp.stack([a[..., :D_FF].reshape(a.shape[:-1] + (N_FF_CHUNKS, FF_CHUNK)),
                                  a[..., D_FF:].reshape(a.shape[:-1] + (N_FF_CHUNKS, FF_CHUNK))], axis=-2)
    w_up_c = halves(w_up).transpose(0, 2, 1, 3, 4).reshape(DEPTH, N_FF_CHUNKS, D_MODEL, 2 * FF_CHUNK).astype(BF16)
    conv_w_c = halves(conv_w).transpose(0, 2, 1, 3, 4).reshape(DEPTH, N_FF_CHUNKS, 3, 2 * FF_CHUNK)
    conv_b_c = halves(conv_b).reshape(DEPTH, N_FF_CHUNKS, 1, 2 * FF_CHUNK)
    w_down_c = w_down.reshape(DEPTH, N_FF_CHUNKS, FF_CHUNK, D_MODEL).astype(BF16)
    col = lambda g: jnp.broadcast_to(g[:, :, None], g.shape + (tm,))
    return {
        "g_attn": g_attn.reshape(DEPTH, 1, D_MODEL), "w_in": w_in_p,
        "gq_t": col(g_q_a), "gk_t": col(g_k_a), "gcq_t": col(g_cq), "w_uq_t": w_uq_t,
        "g_ckv": g_ckv.reshape(DEPTH, 1, KV_LORA), "w_uk": w_uk, "w_uv_t": w_uv_t,
        "g_out": jnp.concatenate([g_out_a, g_out_b], axis=-1).reshape(DEPTH, 1, D_MIX),
        "w_o": w_o.astype(BF16), "g_ffn": g_ffn.reshape(DEPTH, 1, D_MODEL),
        "w_up": w_up_c, "conv_w": conv_w_c, "conv_b": conv_b_c, "w_down": w_down_c,
    }


def _tiles(s):
    return min(512, s), min(1024, s), min(1024, s), min(512, s)


def kernel(x_prompt, x_sample, c_prompt, c_sample, w_ada, b_ada, g_attn, w_in, g_q_a, g_k_a, g_cq, w_uq,
           g_ckv, w_ukv, g_out_a, g_out_b, w_o, g_ffn, w_up, conv_w, conv_b, w_down, g_final):
    n_prompt = x_prompt.shape[0]
    s = x_prompt.shape[1]
    assert x_sample.shape[1] == s and s % GRID_W == 0
    tm, tq, tf, tf_last = _tiles(s)
    assert all(s % t == 0 for t in (tm, tq, tf, tf_last)) and tf % HALO == 0 and tf_last % HALO == 0
    xs = (x_prompt, x_sample)
    c = jnp.concatenate([c_prompt, c_sample], axis=0)
    nb = c.shape[0]

    w = _prepare_weights(g_attn, w_in, g_q_a, g_k_a, g_cq, w_uq, g_ckv, w_ukv, g_out_a, g_out_b, w_o,
                         g_ffn, w_up, conv_w, conv_b, w_down, tm)
    ca, sa = _rope_tables_t(s, HEAD_DIM_A)
    cb, sb = _rope_tables_t(s, QK_ROPE_DIM)
    tabs = {"ca": ca, "sa": sa, "cb": cb, "sb": sb}
    mod_all = _modulation(c, w_ada, b_ada)
    g_fin = g_final.reshape(1, D_MODEL)

    for l in range(DEPTH):
        mod = mod_all[l].reshape(nb, 1, N_MOD * D_MODEL)
        q_all, k_all, v_all = _attn_in(xs, mod, l, w, tabs, tm)
        o_t = _attention(q_all, k_all, v_all, tq)
        x1, h2 = _attn_out(o_t, xs, mod, l, w, tm)
        final = l == DEPTH - 1
        xs = _ffn(h2, x1, mod, l, w, g_fin, tf_last if final else tf, n_prompt, final=final)
    return tuple(xs)
```

```python
import functools
import math

import jax
import jax.numpy as jnp
from jax import lax
from jax.experimental import pallas as pl
from jax.experimental.pallas import tpu as pltpu

F32 = jnp.float32
BF16 = jnp.bfloat16

D_MODEL = 1024
DEPTH = 4
GRID_W = 64
ROPE_THETA = 10000.0
EPS = 1e-6
N_HEADS_A = 8
N_KV_HEADS_A = 2
HEAD_DIM_A = 64
N_HEADS_B = 8
QK_NOPE_DIM = 64
QK_ROPE_DIM = 32
V_DIM_B = 64
Q_LORA = 256
KV_LORA = 128
WIDTH_A = N_HEADS_A * HEAD_DIM_A
WIDTH_B = N_HEADS_B * V_DIM_B
D_MIX = WIDTH_A + WIDTH_B
KV_WIDTH_A = N_KV_HEADS_A * HEAD_DIM_A
D_IN = WIDTH_A + 2 * KV_WIDTH_A + Q_LORA + KV_LORA + QK_ROPE_DIM
D_FF = 2816
N_MOD = 6

LANES = 128
HEAD_SLOT = 128
D_IN_PAD = 1280
ROPE_SLOT_OFF = 64
FF_CHUNK = 256
N_FF_CHUNKS = D_FF // FF_CHUNK
FF_SLABS = 2 * FF_CHUNK // LANES
HALO = 16
N_HEADS = N_HEADS_A + N_HEADS_B
N_KEY_SLABS = 1 + N_HEADS_B
HEAD_DV = HEAD_DIM_A
assert HEAD_DV == V_DIM_B
V_SLOT = HEAD_DV + 16
V_ROWS = (N_KV_HEADS_A + N_HEADS_B) * V_SLOT
QUERY_ITEM = 512
KEY_CHUNK = 256
FOLD_ROWS = 8
SUB_ROWS = 256
LOG2E = math.log2(math.e)
VMEM_LIMIT = 56 * 1024 * 1024


def _rsqrt_mean(ss, n):
    return lax.rsqrt(ss * (1.0 / n) + EPS)


def _mod_kernel(c_ref, w_ref, b_ref, o_ref):
    c = c_ref[...]
    ca = c * (1.0 / (1.0 + jnp.exp(-c)))
    o_ref[0] = jnp.dot(ca.astype(BF16), w_ref[0].astype(BF16), preferred_element_type=F32) + b_ref[0]


def _modulation(c, w_ada, b_ada):
    nb = c.shape[0]
    n_col = N_MOD * D_MODEL
    tn = 1024
    return pl.pallas_call(
        _mod_kernel,
        grid=(DEPTH, n_col // tn),
        in_specs=[
            pl.BlockSpec((nb, D_MODEL), lambda l, j: (0, 0)),
            pl.BlockSpec((1, D_MODEL, tn), lambda l, j: (l, 0, j)),
            pl.BlockSpec((1, 1, tn), lambda l, j: (l, 0, j)),
        ],
        out_specs=pl.BlockSpec((1, nb, tn), lambda l, j: (l, 0, j)),
        out_shape=jax.ShapeDtypeStruct((DEPTH, nb, n_col), F32),
        compiler_params=pltpu.CompilerParams(dimension_semantics=("arbitrary", "arbitrary")),
        name="modulation",
    )(c, w_ada, b_ada.reshape(DEPTH, 1, n_col))


def _rope_t(x, c, s, q):
    sw = jnp.concatenate([x[q:2 * q], x[0:q], x[3 * q:4 * q], x[2 * q:3 * q]], axis=0)
    return x * c + sw * s


def _token_specs(xs, tm, n_i):
    if len(xs) == 1:
        return [pl.BlockSpec((1, tm, D_MODEL), lambda b, i: (b, i, 0))]
    n_p = xs[0].shape[0]
    return [pl.BlockSpec((1, tm, D_MODEL),
                         lambda b, i: (jnp.minimum(b, n_p - 1), jnp.where(b < n_p, i, n_i - 1), 0)),
            pl.BlockSpec((1, tm, D_MODEL),
                         lambda b, i: (jnp.maximum(b - n_p, 0), jnp.where(b >= n_p, i, 0), 0))]


def _load_tokens(x_refs, n_prompt):
    if len(x_refs) == 1:
        return x_refs[0][0]
    return jnp.where(pl.program_id(0) < n_prompt, x_refs[0][0], x_refs[1][0])


def _attn_in_kernel(*refs, n_x, n_prompt):
    x_refs = refs[:n_x]
    (mod_ref, gat_ref, win_ref, gq_ref, gk_ref, gcq_ref, wuqt_ref, gckv_ref, wuk_ref, wuvt_ref,
     ca_ref, sa_ref, cb_ref, sb_ref, q_ref, k_ref, v_ref) = refs[n_x:]
    x_all = _load_tokens(x_refs, n_prompt)
    tm_all = x_all.shape[0]
    tm = min(SUB_ROWS, tm_all)
    mod = mod_ref[0]
    shift = mod[:, 0:D_MODEL]
    scale = mod[:, D_MODEL:2 * D_MODEL]

    def project(t):
        x = x_all[tm * t:tm * (t + 1)]
        r = _rsqrt_mean(jnp.sum(x * x, axis=-1, keepdims=True), D_MODEL)
        h = (x * r) * gat_ref[0] * (1.0 + scale) + shift
        return jnp.dot(h.astype(BF16), win_ref[0], preferred_element_type=F32)

    def finish(t, proj):
        tok = slice(tm * t, tm * (t + 1))
        ca = ca_ref[:, tok]
        sa = sa_ref[:, tok]
        cb = cb_ref[:, tok]
        sb = sb_ref[:, tok]
        zeros_half = jnp.zeros((HEAD_DIM_A, tm), F32)

        o_qa = 0
        q_at = proj[:, o_qa:o_qa + WIDTH_A].T
        qscale_a = (HEAD_DIM_A ** -0.5) * LOG2E
        gq = gq_ref[0, :, tok]
        for j in range(N_HEADS_A):
            xh = q_at[HEAD_DIM_A * j:HEAD_DIM_A * (j + 1)]
            rh = _rsqrt_mean(jnp.sum(xh * xh, axis=0, keepdims=True), HEAD_DIM_A)
            xr = _rope_t((xh * rh) * gq, ca, sa, HEAD_DIM_A // 4) * qscale_a
            g = j // (N_HEADS_A // N_KV_HEADS_A)
            slot = jnp.concatenate([xr, zeros_half] if g == 0 else [zeros_half, xr], axis=0)
            q_ref[0, HEAD_SLOT * j:HEAD_SLOT * (j + 1), tok] = slot.astype(BF16)

        o_ka = WIDTH_A
        k_at = proj[:, o_ka:o_ka + KV_WIDTH_A].T
        gk = gk_ref[0, :, tok]
        kparts = []
        for g in range(N_KV_HEADS_A):
            xh = k_at[HEAD_DIM_A * g:HEAD_DIM_A * (g + 1)]
            rh = _rsqrt_mean(jnp.sum(xh * xh, axis=0, keepdims=True), HEAD_DIM_A)
            kparts.append(_rope_t((xh * rh) * gk, ca, sa, HEAD_DIM_A // 4))
        k_ref[0, 0, tok, :] = jnp.concatenate(kparts, axis=0).T.astype(BF16)
        o_va = o_ka + KV_WIDTH_A
        ones_rows = jnp.where(
            lax.broadcasted_iota(jnp.int32, (V_SLOT - HEAD_DV, tm), 0) == 0, 1.0, 0.0).astype(BF16)

        def store_values(slot, v_t):
            v_ref[0, V_SLOT * slot:V_SLOT * slot + HEAD_DV, tok] = v_t.astype(BF16)
            v_ref[0, V_SLOT * slot + HEAD_DV:V_SLOT * (slot + 1), tok] = ones_rows

        va_t = proj[:, o_va:o_va + KV_WIDTH_A].T
        for g in range(N_KV_HEADS_A):
            store_values(g, va_t[HEAD_DV * g:HEAD_DV * (g + 1)])

        o_cq = o_va + KV_WIDTH_A
        cq_t = proj[:, o_cq:o_cq + Q_LORA].T
        rq = _rsqrt_mean(jnp.sum(cq_t * cq_t, axis=0, keepdims=True), Q_LORA)
        cqn = (cq_t * rq) * gcq_ref[0, :, tok]
        qb_t = jnp.dot(wuqt_ref[0], cqn.astype(BF16), preferred_element_type=F32)
        qscale_b = ((QK_NOPE_DIM + QK_ROPE_DIM) ** -0.5) * LOG2E
        zeros_pad = jnp.zeros((HEAD_SLOT - QK_NOPE_DIM - QK_ROPE_DIM, tm), F32)
        for j in range(N_HEADS_B):
            base = HEAD_SLOT * j
            nope = qb_t[base:base + QK_NOPE_DIM]
            rope = _rope_t(qb_t[base + ROPE_SLOT_OFF:base + ROPE_SLOT_OFF + QK_ROPE_DIM], cb, sb,
                           QK_ROPE_DIM // 4)
            slot = jnp.concatenate([nope, rope, zeros_pad], axis=0) * qscale_b
            row0 = N_HEADS_A * HEAD_SLOT + base
            q_ref[0, row0:row0 + HEAD_SLOT, tok] = slot.astype(BF16)

        o_ckv = o_cq + Q_LORA
        ckv = proj[:, o_ckv:o_ckv + KV_LORA]
        rkv = _rsqrt_mean(jnp.sum(ckv * ckv, axis=-1, keepdims=True), KV_LORA)
        ckvn = (ckv * rkv) * gckv_ref[0]
        k_nope = jnp.dot(ckvn.astype(BF16), wuk_ref[0], preferred_element_type=F32)
        vb_t = jnp.dot(wuvt_ref[0], ckvn.T.astype(BF16), preferred_element_type=F32)
        for j in range(N_HEADS_B):
            store_values(N_KV_HEADS_A + j, vb_t[HEAD_DV * j:HEAD_DV * (j + 1)])
        o_kr = o_ckv + KV_LORA
        kr_t = proj[:, o_kr:o_kr + LANES].T
        kr_rope = _rope_t(kr_t[ROPE_SLOT_OFF:ROPE_SLOT_OFF + QK_ROPE_DIM], cb, sb, QK_ROPE_DIM // 4)
        kr_tile = jnp.concatenate(
            [jnp.zeros((ROPE_SLOT_OFF, tm), F32), kr_rope, zeros_pad], axis=0).T
        for j in range(N_HEADS_B):
            base = HEAD_SLOT * j
            k_ref[0, 1 + j, tok, :] = (k_nope[:, base:base + HEAD_SLOT] + kr_tile).astype(BF16)

    n_sub = tm_all // tm
    projs = [project(t) for t in range(n_sub)]
    for t in range(n_sub):
        finish(t, projs[t])


def _attn_in(xs, mod, l, w, tabs, tm):
    nb = sum(x.shape[0] for x in xs)
    s = xs[0].shape[1]
    grid = (nb, s // tm)
    lay = lambda *blk: pl.BlockSpec((1,) + blk, lambda b, i: (l,) + (0,) * len(blk))
    tab = lambda rows: pl.BlockSpec((rows, tm), lambda b, i: (0, i))
    out_shapes = (
        jax.ShapeDtypeStruct((nb, N_HEADS * HEAD_SLOT, s), BF16),
        jax.ShapeDtypeStruct((nb, N_KEY_SLABS, s, HEAD_SLOT), BF16),
        jax.ShapeDtypeStruct((nb, V_ROWS, s), BF16),
    )
    return pl.pallas_call(
        functools.partial(_attn_in_kernel, n_x=len(xs), n_prompt=xs[0].shape[0]),
        grid=grid,
        in_specs=_token_specs(xs, tm, s // tm) + [
            pl.BlockSpec((1, 1, N_MOD * D_MODEL), lambda b, i: (b, 0, 0)),
            lay(1, D_MODEL),
            lay(D_MODEL, D_IN_PAD),
            lay(HEAD_DIM_A, tm),
            lay(HEAD_DIM_A, tm),
            lay(Q_LORA, tm),
            lay(N_HEADS_B * HEAD_SLOT, Q_LORA),
            lay(1, KV_LORA),
            lay(KV_LORA, N_HEADS_B * HEAD_SLOT),
            lay(WIDTH_B, KV_LORA),
            tab(HEAD_DIM_A), tab(HEAD_DIM_A), tab(QK_ROPE_DIM), tab(QK_ROPE_DIM),
        ],
        out_specs=(pl.BlockSpec((1, N_HEADS * HEAD_SLOT, tm), lambda b, i: (b, 0, i)),
                   pl.BlockSpec((1, N_KEY_SLABS, tm, HEAD_SLOT), lambda b, i: (b, 0, i, 0)),
                   pl.BlockSpec((1, V_ROWS, tm), lambda b, i: (b, 0, i))),
        out_shape=out_shapes,
        compiler_params=pltpu.CompilerParams(
            dimension_semantics=("arbitrary", "arbitrary"), vmem_limit_bytes=VMEM_LIMIT),
        name="attn_in",
    )(*xs, mod, w["g_attn"], w["w_in"], w["gq_t"], w["gk_t"], w["gcq_t"], w["w_uq_t"], w["g_ckv"],
      w["w_uk"], w["w_uv_t"], tabs["ca"], tabs["sa"], tabs["cb"], tabs["sb"])


def _fold_rows(x, op):
    parts = [x[FOLD_ROWS * i:FOLD_ROWS * (i + 1)] for i in range(x.shape[0] // FOLD_ROWS)]
    while len(parts) > 1:
        parts = [op(parts[2 * i], parts[2 * i + 1]) for i in range(len(parts) // 2)]
    return parts[0]


def _key_slab(h):
    if isinstance(h, int):
        return 0 if h < N_HEADS_A else h - N_HEADS_A + 1
    return jnp.where(h < N_HEADS_A, 0, h - N_HEADS_A + 1)


def _value_row(h):
    per_kv = N_HEADS_A // N_KV_HEADS_A
    if isinstance(h, int):
        return V_SLOT * (h // per_kv if h < N_HEADS_A else h - N_HEADS_A + N_KV_HEADS_A)
    return V_SLOT * jnp.where(h < N_HEADS_A, h // per_kv, h - N_HEADS_A + N_KV_HEADS_A)


def _row_block(start, size):
    if isinstance(start, int):
        return pl.ds(start, size)
    return pl.ds(pl.multiple_of(start, size), size)


def _attention_kernel(q_ref, k_ref, v_ref, ot_ref, s0_ref, s1_ref):
    n_keys = k_ref.shape[2]
    w = s0_ref.shape[1]
    n_parts = ot_ref.shape[2] // w
    n_chunks = n_keys // KEY_CHUNK
    rows = lambda c: slice(KEY_CHUNK * c, KEY_CHUNK * (c + 1))
    cols = lambda t: slice(w * t, w * (t + 1))

    def stage(nxt, s_next_ref, cur, s_cur_ref, m_cur):
        m_next = None
        if nxt is not None:
            h_next, t_next = nxt
            q_t = q_ref[0, _row_block(h_next * HEAD_SLOT, HEAD_SLOT), cols(t_next)]
            slab = _key_slab(h_next)
        if cur is not None:
            h_cur, t_cur = cur
            m_b = jnp.broadcast_to(jnp.max(m_cur, axis=0, keepdims=True), (KEY_CHUNK, w))
            v_rows = _row_block(_value_row(h_cur), V_SLOT)
            o_acc = jnp.zeros((V_SLOT, w), F32)
        for c in range(n_chunks):
            if nxt is not None:
                s = jnp.dot(k_ref[0, slab, rows(c), :], q_t, preferred_element_type=F32)
                s_next_ref[rows(c), :] = s
                folded = _fold_rows(s, jnp.maximum)
                m_next = folded if m_next is None else jnp.maximum(m_next, folded)
            if cur is not None:
                p = jnp.exp2(s_cur_ref[rows(c), :] - m_b)
                o_acc = o_acc + jnp.dot(v_ref[0, v_rows, rows(c)], p.astype(BF16),
                                        preferred_element_type=F32)
        if cur is not None:
            denom = o_acc[HEAD_DV:HEAD_DV + 1]
            ot_ref[0, _row_block(h_cur * HEAD_DV, HEAD_DV), cols(t_cur)] = (
                o_acc[:HEAD_DV] * (1.0 / denom)).astype(ot_ref.dtype)
        return m_next

    bufs = (s0_ref, s1_ref)

    def head(h, m, last=False):
        for t in range(n_parts):
            nxt = (h, t + 1) if t + 1 < n_parts else (None if last else (h + 1, 0))
            m = stage(nxt, bufs[(t + 1) % 2] if nxt is not None else None, (h, t), bufs[t % 2], m)
        return m

    m = lax.fori_loop(0, N_HEADS - 1, head, stage((0, 0), s0_ref, None, None, None))
    head(N_HEADS - 1, m, last=True)


def _attention(q_all, k_all, v_all, tq):
    nb, _, s, _ = k_all.shape
    w = min(QUERY_ITEM, tq // 2)
    assert s % KEY_CHUNK == 0 and w % LANES == 0 and tq % (2 * w) == 0
    score_buf = pltpu.VMEM((s, w), F32)
    return pl.pallas_call(
        _attention_kernel,
        grid=(nb, s // tq),
        in_specs=[pl.BlockSpec((1, N_HEADS * HEAD_SLOT, tq), lambda b, i: (b, 0, i)),
                  pl.BlockSpec((1, N_KEY_SLABS, s, HEAD_SLOT), lambda b, i: (b, 0, 0, 0)),
                  pl.BlockSpec((1, V_ROWS, s), lambda b, i: (b, 0, 0))],
        out_specs=pl.BlockSpec((1, D_MIX, tq), lambda b, i: (b, 0, i)),
        out_shape=jax.ShapeDtypeStruct((nb, D_MIX, s), BF16),
        scratch_shapes=[score_buf, score_buf],
        compiler_params=pltpu.CompilerParams(
            dimension_semantics=("arbitrary", "arbitrary"), vmem_limit_bytes=VMEM_LIMIT),
        name="attention",
    )(q_all, k_all, v_all)


def _attn_out_kernel(ot_ref, *refs, n_x, n_prompt):
    x_refs = refs[:n_x]
    mod_ref, go_ref, wo_ref, gffn_ref, x1_ref, h2_ref = refs[n_x:]
    o_t = ot_ref[0].astype(F32)
    oa = o_t[:WIDTH_A]
    ob = o_t[WIDTH_A:]
    ra = _rsqrt_mean(jnp.sum(oa * oa, axis=0, keepdims=True), WIDTH_A)
    rb = _rsqrt_mean(jnp.sum(ob * ob, axis=0, keepdims=True), WIDTH_B)
    on = jnp.concatenate([oa * ra, ob * rb], axis=0).T * go_ref[0]
    res = jnp.dot(on.astype(BF16), wo_ref[0], preferred_element_type=F32)
    mod = mod_ref[0]
    gate_a = mod[:, 2 * D_MODEL:3 * D_MODEL]
    shift_f = mod[:, 3 * D_MODEL:4 * D_MODEL]
    scale_f = mod[:, 4 * D_MODEL:5 * D_MODEL]
    x1 = _load_tokens(x_refs, n_prompt) + gate_a * res
    x1_ref[0] = x1
    r = _rsqrt_mean(jnp.sum(x1 * x1, axis=-1, keepdims=True), D_MODEL)
    h2_ref[0] = ((x1 * r) * gffn_ref[0] * (1.0 + scale_f) + shift_f).astype(BF16)


def _attn_out(o_t, xs, mod, l, w, tm):
    nb, _, s = o_t.shape
    lay = lambda *blk: pl.BlockSpec((1,) + blk, lambda b, i: (l,) + (0,) * len(blk))
    tok = pl.BlockSpec((1, tm, D_MODEL), lambda b, i: (b, i, 0))
    return pl.pallas_call(
        functools.partial(_attn_out_kernel, n_x=len(xs), n_prompt=xs[0].shape[0]),
        grid=(nb, s // tm),
        in_specs=[pl.BlockSpec((1, D_MIX, tm), lambda b, i: (b, 0, i))] + _token_specs(xs, tm, s // tm) + [
            pl.BlockSpec((1, 1, N_MOD * D_MODEL), lambda b, i: (b, 0, 0)),
            lay(1, D_MIX),
            lay(D_MIX, D_MODEL),
            lay(1, D_MODEL),
        ],
        out_specs=(tok, tok),
        out_shape=(jax.ShapeDtypeStruct((nb, s, D_MODEL), F32), jax.ShapeDtypeStruct((nb, s, D_MODEL), BF16)),
        compiler_params=pltpu.CompilerParams(
            dimension_semantics=("arbitrary", "arbitrary"), vmem_limit_bytes=VMEM_LIMIT),
        name="attn_out",
    )(o_t, *xs, mod, w["g_out"], w["w_o"], w["g_ffn"])


def _ffn_kernel(hm_ref, hp_ref, hn_ref, x1_ref, mod_ref, wup_ref, cw_ref, cb_ref, wdn_ref, gfin_ref,
                *refs, final, n_prompt):
    n_out = 2 if final else 1
    out_refs = refs[:n_out]
    hext_ref, ua_ref, ub_ref = refs[n_out:n_out + 3]
    acc_ref = refs[n_out + 3] if final else out_refs[0].at[0]
    i = pl.program_id(1)
    n_i = pl.num_programs(1)
    tm = hm_ref.shape[1]

    keep_prev = jnp.where(i > 0, 1.0, 0.0).astype(F32)
    keep_next = jnp.where(i < n_i - 1, 1.0, 0.0).astype(F32)
    hext_ref[0:HALO, :] = (hp_ref[0].astype(F32) * keep_prev).astype(BF16)
    hext_ref[HALO:HALO + tm, :] = hm_ref[0]
    hext_ref[HALO + tm:HALO + tm + HALO, :] = (hn_ref[0].astype(F32) * keep_next).astype(BF16)
    acc_ref[...] = jnp.zeros((tm, D_MODEL), F32)

    def up(c, u_ref):
        u = jnp.dot(hext_ref[...], wup_ref[0, c], preferred_element_type=F32)
        for j in range(FF_SLABS):
            u_ref[j] = u[:, LANES * j:LANES * (j + 1)]

    def mix(c, u_ref):
        cw = cw_ref[0, c]
        cb = cb_ref[0, c]
        ys = []
        for j in range(FF_SLABS):
            lanes = slice(LANES * j, LANES * (j + 1))
            ys.append(u_ref[j, HALO - 1:HALO - 1 + tm, :] * cw[0:1, lanes]
                      + u_ref[j, HALO:HALO + tm, :] * cw[1:2, lanes]
                      + u_ref[j, HALO + 1:HALO + 1 + tm, :] * cw[2:3, lanes] + cb[:, lanes])
        half = FF_SLABS // 2
        acts = []
        for j in range(half):
            gate = ys[half + j]
            acts.append((gate * (1.0 / (1.0 + jnp.exp(-gate)))) * ys[j])
        act = jnp.concatenate(acts, axis=1).astype(BF16)
        acc_ref[...] += jnp.dot(act, wdn_ref[0, c], preferred_element_type=F32)

    def pair(k, carry):
        c = 2 * k
        up(c + 1, ub_ref)
        mix(c, ua_ref)
        up(c + 2, ua_ref)
        mix(c + 1, ub_ref)
        return carry

    assert N_FF_CHUNKS % 2 == 1
    up(0, ua_ref)
    lax.fori_loop(0, N_FF_CHUNKS // 2, pair, 0)
    mix(N_FF_CHUNKS - 1, ua_ref)

    gate_f = mod_ref[0][:, 5 * D_MODEL:6 * D_MODEL]
    x2 = x1_ref[0] + gate_f * acc_ref[...]
    if final:
        r = _rsqrt_mean(jnp.sum(x2 * x2, axis=-1, keepdims=True), D_MODEL)
        x2 = (x2 * r) * gfin_ref[...]
    if not final:
        out_refs[0][0] = x2
    else:
        is_prompt = pl.program_id(0) < n_prompt

        @pl.when(is_prompt)
        def _():
            out_refs[0][0] = x2

        @pl.when(jnp.logical_not(is_prompt))
        def _():
            out_refs[1][0] = x2


def _ffn(h2, x1, mod, l, w, g_final, tm, n_prompt, final):
    nb, s, _ = x1.shape
    n_i = s // tm
    if final:
        out_shape = (jax.ShapeDtypeStruct((n_prompt, s, D_MODEL), F32),
                     jax.ShapeDtypeStruct((nb - n_prompt, s, D_MODEL), F32))
    else:
        out_shape = (jax.ShapeDtypeStruct((nb, s, D_MODEL), F32),)
    hb = tm // HALO
    tok = pl.BlockSpec((1, tm, D_MODEL), lambda b, i: (b, i, 0))
    resident = lambda *blk: pl.BlockSpec((1,) + blk, lambda b, i: (l,) + (0,) * len(blk),
                                         pipeline_mode=pl.Buffered(1))
    u_buf = pltpu.VMEM((FF_SLABS, tm + 2 * HALO, LANES), F32)
    return pl.pallas_call(
        functools.partial(_ffn_kernel, final=final, n_prompt=n_prompt),
        grid=(nb, n_i),
        in_specs=[
            tok,
            pl.BlockSpec((1, HALO, D_MODEL), lambda b, i: (b, jnp.maximum(i * hb - 1, 0), 0)),
            pl.BlockSpec((1, HALO, D_MODEL), lambda b, i: (b, jnp.minimum((i + 1) * hb, s // HALO - 1), 0)),
            tok,
            pl.BlockSpec((1, 1, N_MOD * D_MODEL), lambda b, i: (b, 0, 0)),
            resident(N_FF_CHUNKS, D_MODEL, 2 * FF_CHUNK),
            resident(N_FF_CHUNKS, 3, 2 * FF_CHUNK),
            resident(N_FF_CHUNKS, 1, 2 * FF_CHUNK),
            resident(N_FF_CHUNKS, FF_CHUNK, D_MODEL),
            pl.BlockSpec((1, D_MODEL), lambda b, i: (0, 0)),
        ],
        out_specs=tuple(_token_specs(out_shape, tm, n_i)),
        out_shape=out_shape,
        scratch_shapes=[pltpu.VMEM((tm + 2 * HALO, D_MODEL), BF16), u_buf, u_buf]
        + ([pltpu.VMEM((tm, D_MODEL), F32)] if final else []),
        compiler_params=pltpu.CompilerParams(
            dimension_semantics=("arbitrary", "arbitrary"), vmem_limit_bytes=VMEM_LIMIT),
        name="ffn",
    )(h2, h2, h2, x1, mod, w["w_up"], w["conv_w"], w["conv_b"], w["w_down"], g_final)


def _rope_tables_t(n_tokens, rot_dim):
    rows = n_tokens // GRID_W
    row = jnp.repeat(jnp.arange(rows, dtype=F32), GRID_W)
    col = jnp.tile(jnp.arange(GRID_W, dtype=F32), rows)
    quarter = rot_dim // 4
    inv_freq = ROPE_THETA ** (-jnp.arange(quarter, dtype=F32) / quarter)
    ang = jnp.stack([row, col], axis=-1)[:, :, None] * inv_freq
    cos, sin = jnp.cos(ang), jnp.sin(ang)
    c = jnp.concatenate([cos[:, 0], cos[:, 0], cos[:, 1], cos[:, 1]], axis=-1).T
    s = jnp.concatenate([-sin[:, 0], sin[:, 0], -sin[:, 1], sin[:, 1]], axis=-1).T
    return c, s


def _prepare_weights(g_attn, w_in, g_q_a, g_k_a, g_cq, w_uq, g_ckv, w_ukv, g_out_a, g_out_b, w_o,
                     g_ffn, w_up, conv_w, conv_b, w_down, tm):
    o_kr = D_IN - QK_ROPE_DIM
    w_in_p = jnp.concatenate([
        w_in[..., :o_kr], jnp.zeros((DEPTH, D_MODEL, ROPE_SLOT_OFF), F32), w_in[..., o_kr:],
        jnp.zeros((DEPTH, D_MODEL, LANES - ROPE_SLOT_OFF - QK_ROPE_DIM), F32)], axis=-1).astype(BF16)
    hq = QK_NOPE_DIM + QK_ROPE_DIM
    w_uq_p = jnp.pad(w_uq.reshape(DEPTH, Q_LORA, N_HEADS_B, hq), ((0, 0), (0, 0), (0, 0), (0, HEAD_SLOT - hq)))
    w_uq_t = w_uq_p.reshape(DEPTH, Q_LORA, N_HEADS_B * HEAD_SLOT).transpose(0, 2, 1).astype(BF16)
    w_ukv_h = w_ukv.reshape(DEPTH, KV_LORA, N_HEADS_B, QK_NOPE_DIM + V_DIM_B)
    w_uk = jnp.pad(w_ukv_h[..., :QK_NOPE_DIM], ((0, 0), (0, 0), (0, 0), (0, HEAD_SLOT - QK_NOPE_DIM)))
    w_uk = w_uk.reshape(DEPTH, KV_LORA, N_HEADS_B * HEAD_SLOT).astype(BF16)
    w_uv_t = w_ukv_h[..., QK_NOPE_DIM:].reshape(DEPTH, KV_LORA, WIDTH_B).transpose(0, 2, 1).astype(BF16)
    halves = lambda a: jnp.stack([a[..., :D_FF].reshape(a.shape[:-1] + (N_FF_CHUNKS, FF_CHUNK)),
                                  a[..., D_FF:].reshape(a.shape[:-1] + (N_FF_CHUNKS, FF_CHUNK))], axis=-2)
    w_up_c = halves(w_up).transpose(0, 2, 1, 3, 4).reshape(DEPTH, N_FF_CHUNKS, D_MODEL, 2 * FF_CHUNK).astype(BF16)
    conv_w_c = halves(conv_w).transpose(0, 2, 1, 3, 4).reshape(DEPTH, N_FF_CHUNKS, 3, 2 * FF_CHUNK)
    conv_b_c = halves(conv_b).reshape(DEPTH, N_FF_CHUNKS, 1, 2 * FF_CHUNK)
    w_down_c = w_down.reshape(DEPTH, N_FF_CHUNKS, FF_CHUNK, D_MODEL).astype(BF16)
    col = lambda g: jnp.broadcast_to(g[:, :, None], g.shape + (tm,))
    return {
        "g_attn": g_attn.reshape(DEPTH, 1, D_MODEL), "w_in": w_in_p,
        "gq_t": col(g_q_a), "gk_t": col(g_k_a), "gcq_t": col(g_cq), "w_uq_t": w_uq_t,
        "g_ckv": g_ckv.reshape(DEPTH, 1, KV_LORA), "w_uk": w_uk, "w_uv_t": w_uv_t,
        "g_out": jnp.concatenate([g_out_a, g_out_b], axis=-1).reshape(DEPTH, 1, D_MIX),
        "w_o": w_o.astype(BF16), "g_ffn": g_ffn.reshape(DEPTH, 1, D_MODEL),
        "w_up": w_up_c, "conv_w": conv_w_c, "conv_b": conv_b_c, "w_down": w_down_c,
    }


def _tiles(s):
    return min(1024, s), min(2048, s), min(1024, s), min(512, s)


def kernel(x_prompt, x_sample, c_prompt, c_sample, w_ada, b_ada, g_attn, w_in, g_q_a, g_k_a, g_cq, w_uq,
           g_ckv, w_ukv, g_out_a, g_out_b, w_o, g_ffn, w_up, conv_w, conv_b, w_down, g_final):
    n_prompt = x_prompt.shape[0]
    s = x_prompt.shape[1]
    assert x_sample.shape[1] == s and s % GRID_W == 0
    tm, tq, tf, tf_last = _tiles(s)
    assert all(s % t == 0 for t in (tm, tq, tf, tf_last)) and tf % HALO == 0 and tf_last % HALO == 0
    xs = (x_prompt, x_sample)
    c = jnp.concatenate([c_prompt, c_sample], axis=0)
    nb = c.shape[0]

    w = _prepare_weights(g_attn, w_in, g_q_a, g_k_a, g_cq, w_uq, g_ckv, w_ukv, g_out_a, g_out_b, w_o,
                         g_ffn, w_up, conv_w, conv_b, w_down, tm)
    ca, sa = _rope_tables_t(s, HEAD_DIM_A)
    cb, sb = _rope_tables_t(s, QK_ROPE_DIM)
    tabs = {"ca": ca, "sa": sa, "cb": cb, "sb": sb}
    mod_all = _modulation(c, w_ada, b_ada)
    g_fin = g_final.reshape(1, D_MODEL)

    for l in range(DEPTH):
        mod = mod_all[l].reshape(nb, 1, N_MOD * D_MODEL)
        q_all, k_all, v_all = _attn_in(xs, mod, l, w, tabs, tm)
        o_t = _attention(q_all, k_all, v_all, tq)
        x1, h2 = _attn_out(o_t, xs, mod, l, w, tm)
        final = l == DEPTH - 1
        xs = _ffn(h2, x1, mod, l, w, g_fin, tf_last if final else tf, n_prompt, final=final)
    return tuple(xs)
```

```python
import functools
import math

import jax
import jax.numpy as jnp
from jax import lax
from jax.experimental import pallas as pl
from jax.experimental.pallas import tpu as pltpu

F32 = jnp.float32
BF16 = jnp.bfloat16

D_MODEL = 1024
DEPTH = 4
GRID_W = 64
ROPE_THETA = 10000.0
EPS = 1e-6
N_HEADS_A = 8
N_KV_HEADS_A = 2
HEAD_DIM_A = 64
N_HEADS_B = 8
QK_NOPE_DIM = 64
QK_ROPE_DIM = 32
V_DIM_B = 64
Q_LORA = 256
KV_LORA = 128
WIDTH_A = N_HEADS_A * HEAD_DIM_A
WIDTH_B = N_HEADS_B * V_DIM_B
D_MIX = WIDTH_A + WIDTH_B
KV_WIDTH_A = N_KV_HEADS_A * HEAD_DIM_A
D_IN = WIDTH_A + 2 * KV_WIDTH_A + Q_LORA + KV_LORA + QK_ROPE_DIM
D_FF = 2816
N_MOD = 6

LANES = 128
HEAD_SLOT = 128
D_IN_PAD = 1280
ROPE_SLOT_OFF = 64
FF_CHUNK = 256
N_FF_CHUNKS = D_FF // FF_CHUNK
FF_SLABS = 2 * FF_CHUNK // LANES
HALO = 16
N_HEADS = N_HEADS_A + N_HEADS_B
N_KEY_SLABS = 1 + N_HEADS_B
HEAD_DV = HEAD_DIM_A
assert HEAD_DV == V_DIM_B
V_SLOT = HEAD_DV + 16
V_ROWS = (N_KV_HEADS_A + N_HEADS_B) * V_SLOT
QUERY_ITEM = 512
KEY_CHUNK = 256
FOLD_ROWS = 8
SUB_ROWS = 256
LOG2E = math.log2(math.e)
VMEM_LIMIT = 56 * 1024 * 1024


def _rsqrt_mean(ss, n):
    return lax.rsqrt(ss * (1.0 / n) + EPS)


def _mod_kernel(c_ref, w_ref, b_ref, o_ref):
    c = c_ref[...]
    ca = c * (1.0 / (1.0 + jnp.exp(-c)))
    o_ref[0] = jnp.dot(ca.astype(BF16), w_ref[0].astype(BF16), preferred_element_type=F32) + b_ref[0]


def _modulation(c, w_ada, b_ada):
    nb = c.shape[0]
    n_col = N_MOD * D_MODEL
    tn = 1024
    return pl.pallas_call(
        _mod_kernel,
        grid=(DEPTH, n_col // tn),
        in_specs=[
            pl.BlockSpec((nb, D_MODEL), lambda l, j: (0, 0)),
            pl.BlockSpec((1, D_MODEL, tn), lambda l, j: (l, 0, j)),
            pl.BlockSpec((1, 1, tn), lambda l, j: (l, 0, j)),
        ],
        out_specs=pl.BlockSpec((1, nb, tn), lambda l, j: (l, 0, j)),
        out_shape=jax.ShapeDtypeStruct((DEPTH, nb, n_col), F32),
        compiler_params=pltpu.CompilerParams(dimension_semantics=("arbitrary", "arbitrary")),
        name="modulation",
    )(c, w_ada, b_ada.reshape(DEPTH, 1, n_col))


def _rope_t(x, c, s, q):
    sw = jnp.concatenate([x[q:2 * q], x[0:q], x[3 * q:4 * q], x[2 * q:3 * q]], axis=0)
    return x * c + sw * s


def _token_specs(xs, tm, n_i):
    if len(xs) == 1:
        return [pl.BlockSpec((1, tm, D_MODEL), lambda b, i: (b, i, 0))]
    n_p = xs[0].shape[0]
    return [pl.BlockSpec((1, tm, D_MODEL),
                         lambda b, i: (jnp.minimum(b, n_p - 1), jnp.where(b < n_p, i, n_i - 1), 0)),
            pl.BlockSpec((1, tm, D_MODEL),
                         lambda b, i: (jnp.maximum(b - n_p, 0), jnp.where(b >= n_p, i, 0), 0))]


def _load_tokens(x_refs, n_prompt):
    if len(x_refs) == 1:
        return x_refs[0][0]
    return jnp.where(pl.program_id(0) < n_prompt, x_refs[0][0], x_refs[1][0])


def _attn_in_kernel(*refs, n_x, n_prompt):
    x_refs = refs[:n_x]
    (mod_ref, gat_ref, win_ref, gq_ref, gk_ref, gcq_ref, wuqt_ref, gckv_ref, wuk_ref, wuvt_ref,
     ca_ref, sa_ref, cb_ref, sb_ref, q_ref, k_ref, v_ref) = refs[n_x:]
    x_all = _load_tokens(x_refs, n_prompt)
    tm_all = x_all.shape[0]
    tm = min(SUB_ROWS, tm_all)
    mod = mod_ref[0]
    shift = mod[:, 0:D_MODEL]
    scale = mod[:, D_MODEL:2 * D_MODEL]

    def project(t):
        x = x_all[tm * t:tm * (t + 1)]
        r = _rsqrt_mean(jnp.sum(x * x, axis=-1, keepdims=True), D_MODEL)
        h = (x * r) * gat_ref[0] * (1.0 + scale) + shift
        return jnp.dot(h.astype(BF16), win_ref[0], preferred_element_type=F32)

    def finish(t, proj):
        tok = slice(tm * t, tm * (t + 1))
        ca = ca_ref[:, tok]
        sa = sa_ref[:, tok]
        cb = cb_ref[:, tok]
        sb = sb_ref[:, tok]
        zeros_half = jnp.zeros((HEAD_DIM_A, tm), F32)

        o_qa = 0
        q_at = proj[:, o_qa:o_qa + WIDTH_A].T
        qscale_a = (HEAD_DIM_A ** -0.5) * LOG2E
        gq = gq_ref[0, :, tok]
        for j in range(N_HEADS_A):
            xh = q_at[HEAD_DIM_A * j:HEAD_DIM_A * (j + 1)]
            rh = _rsqrt_mean(jnp.sum(xh * xh, axis=0, keepdims=True), HEAD_DIM_A)
            xr = _rope_t((xh * rh) * gq, ca, sa, HEAD_DIM_A // 4) * qscale_a
            g = j // (N_HEADS_A // N_KV_HEADS_A)
            slot = jnp.concatenate([xr, zeros_half] if g == 0 else [zeros_half, xr], axis=0)
            q_ref[0, HEAD_SLOT * j:HEAD_SLOT * (j + 1), tok] = slot.astype(BF16)

        o_ka = WIDTH_A
        k_at = proj[:, o_ka:o_ka + KV_WIDTH_A].T
        gk = gk_ref[0, :, tok]
        kparts = []
        for g in range(N_KV_HEADS_A):
            xh = k_at[HEAD_DIM_A * g:HEAD_DIM_A * (g + 1)]
            rh = _rsqrt_mean(jnp.sum(xh * xh, axis=0, keepdims=True), HEAD_DIM_A)
            kparts.append(_rope_t((xh * rh) * gk, ca, sa, HEAD_DIM_A // 4))
        k_ref[0, 0, tok, :] = jnp.concatenate(kparts, axis=0).T.astype(BF16)
        o_va = o_ka + KV_WIDTH_A
        ones_rows = jnp.where(
            lax.broadcasted_iota(jnp.int32, (V_SLOT - HEAD_DV, tm), 0) == 0, 1.0, 0.0).astype(BF16)

        def store_values(slot, v_t):
            v_ref[0, V_SLOT * slot:V_SLOT * slot + HEAD_DV, tok] = v_t.astype(BF16)
            v_ref[0, V_SLOT * slot + HEAD_DV:V_SLOT * (slot + 1), tok] = ones_rows

        va_t = proj[:, o_va:o_va + KV_WIDTH_A].T
        for g in range(N_KV_HEADS_A):
            store_values(g, va_t[HEAD_DV * g:HEAD_DV * (g + 1)])

        o_cq = o_va + KV_WIDTH_A
        cq_t = proj[:, o_cq:o_cq + Q_LORA].T
        rq = _rsqrt_mean(jnp.sum(cq_t * cq_t, axis=0, keepdims=True), Q_LORA)
        cqn = (cq_t * rq) * gcq_ref[0, :, tok]
        qb_t = jnp.dot(wuqt_ref[0], cqn.astype(BF16), preferred_element_type=F32)
        qscale_b = ((QK_NOPE_DIM + QK_ROPE_DIM) ** -0.5) * LOG2E
        zeros_pad = jnp.zeros((HEAD_SLOT - QK_NOPE_DIM - QK_ROPE_DIM, tm), F32)
        for j in range(N_HEADS_B):
            base = HEAD_SLOT * j
            nope = qb_t[base:base + QK_NOPE_DIM]
            rope = _rope_t(qb_t[base + ROPE_SLOT_OFF:base + ROPE_SLOT_OFF + QK_ROPE_DIM], cb, sb,
                           QK_ROPE_DIM // 4)
            slot = jnp.concatenate([nope, rope, zeros_pad], axis=0) * qscale_b
            row0 = N_HEADS_A * HEAD_SLOT + base
            q_ref[0, row0:row0 + HEAD_SLOT, tok] = slot.astype(BF16)

        o_ckv = o_cq + Q_LORA
        ckv = proj[:, o_ckv:o_ckv + KV_LORA]
        rkv = _rsqrt_mean(jnp.sum(ckv * ckv, axis=-1, keepdims=True), KV_LORA)
        ckvn = (ckv * rkv) * gckv_ref[0]
        k_nope = jnp.dot(ckvn.astype(BF16), wuk_ref[0], preferred_element_type=F32)
        vb_t = jnp.dot(wuvt_ref[0], ckvn.T.astype(BF16), preferred_element_type=F32)
        for j in range(N_HEADS_B):
            store_values(N_KV_HEADS_A + j, vb_t[HEAD_DV * j:HEAD_DV * (j + 1)])
        o_kr = o_ckv + KV_LORA
        kr_t = proj[:, o_kr:o_kr + LANES].T
        kr_rope = _rope_t(kr_t[ROPE_SLOT_OFF:ROPE_SLOT_OFF + QK_ROPE_DIM], cb, sb, QK_ROPE_DIM // 4)
        kr_tile = jnp.concatenate(
            [jnp.zeros((ROPE_SLOT_OFF, tm), F32), kr_rope, zeros_pad], axis=0).T
        for j in range(N_HEADS_B):
            base = HEAD_SLOT * j
            k_ref[0, 1 + j, tok, :] = (k_nope[:, base:base + HEAD_SLOT] + kr_tile).astype(BF16)

    n_sub = tm_all // tm
    projs = [project(t) for t in range(n_sub)]
    for t in range(n_sub):
        finish(t, projs[t])


def _attn_in(xs, mod, l, w, tabs, tm):
    nb = sum(x.shape[0] for x in xs)
    s = xs[0].shape[1]
    grid = (nb, s // tm)
    lay = lambda *blk: pl.BlockSpec((1,) + blk, lambda b, i: (l,) + (0,) * len(blk))
    tab = lambda rows: pl.BlockSpec((rows, tm), lambda b, i: (0, i))
    out_shapes = (
        jax.ShapeDtypeStruct((nb, N_HEADS * HEAD_SLOT, s), BF16),
        jax.ShapeDtypeStruct((nb, N_KEY_SLABS, s, HEAD_SLOT), BF16),
        jax.ShapeDtypeStruct((nb, V_ROWS, s), BF16),
    )
    return pl.pallas_call(
        functools.partial(_attn_in_kernel, n_x=len(xs), n_prompt=xs[0].shape[0]),
        grid=grid,
        in_specs=_token_specs(xs, tm, s // tm) + [
            pl.BlockSpec((1, 1, N_MOD * D_MODEL), lambda b, i: (b, 0, 0)),
            lay(1, D_MODEL),
            lay(D_MODEL, D_IN_PAD),
            lay(HEAD_DIM_A, tm),
            lay(HEAD_DIM_A, tm),
            lay(Q_LORA, tm),
            lay(N_HEADS_B * HEAD_SLOT, Q_LORA),
            lay(1, KV_LORA),
            lay(KV_LORA, N_HEADS_B * HEAD_SLOT),
            lay(WIDTH_B, KV_LORA),
            tab(HEAD_DIM_A), tab(HEAD_DIM_A), tab(QK_ROPE_DIM), tab(QK_ROPE_DIM),
        ],
        out_specs=(pl.BlockSpec((1, N_HEADS * HEAD_SLOT, tm), lambda b, i: (b, 0, i)),
                   pl.BlockSpec((1, N_KEY_SLABS, tm, HEAD_SLOT), lambda b, i: (b, 0, i, 0)),
                   pl.BlockSpec((1, V_ROWS, tm), lambda b, i: (b, 0, i))),
        out_shape=out_shapes,
        compiler_params=pltpu.CompilerParams(
            dimension_semantics=("arbitrary", "arbitrary"), vmem_limit_bytes=VMEM_LIMIT),
        name="attn_in",
    )(*xs, mod, w["g_attn"], w["w_in"], w["gq_t"], w["gk_t"], w["gcq_t"], w["w_uq_t"], w["g_ckv"],
      w["w_uk"], w["w_uv_t"], tabs["ca"], tabs["sa"], tabs["cb"], tabs["sb"])


def _fold_rows(x, op):
    parts = [x[FOLD_ROWS * i:FOLD_ROWS * (i + 1)] for i in range(x.shape[0] // FOLD_ROWS)]
    while len(parts) > 1:
        parts = [op(parts[2 * i], parts[2 * i + 1]) for i in range(len(parts) // 2)]
    return parts[0]


def _key_slab(h):
    if isinstance(h, int):
        return 0 if h < N_HEADS_A else h - N_HEADS_A + 1
    return jnp.where(h < N_HEADS_A, 0, h - N_HEADS_A + 1)


def _value_row(h):
    per_kv = N_HEADS_A // N_KV_HEADS_A
    if isinstance(h, int):
        return V_SLOT * (h // per_kv if h < N_HEADS_A else h - N_HEADS_A + N_KV_HEADS_A)
    return V_SLOT * jnp.where(h < N_HEADS_A, h // per_kv, h - N_HEADS_A + N_KV_HEADS_A)


def _row_block(start, size):
    if isinstance(start, int):
        return pl.ds(start, size)
    return pl.ds(pl.multiple_of(start, size), size)


def _attention_kernel(q_ref, k_ref, v_ref, ot_ref, s0_ref, s1_ref):
    n_keys = k_ref.shape[2]
    w = s0_ref.shape[1]
    n_parts = ot_ref.shape[2] // w
    n_chunks = n_keys // KEY_CHUNK
    rows = lambda c: slice(KEY_CHUNK * c, KEY_CHUNK * (c + 1))
    cols = lambda t: slice(w * t, w * (t + 1))

    def stage(nxt, s_next_ref, cur, s_cur_ref, m_cur):
        m_next = None
        if nxt is not None:
            h_next, t_next = nxt
            q_t = q_ref[0, _row_block(h_next * HEAD_SLOT, HEAD_SLOT), cols(t_next)]
            slab = _key_slab(h_next)
        if cur is not None:
            h_cur, t_cur = cur
            m_b = jnp.broadcast_to(jnp.max(m_cur, axis=0, keepdims=True), (KEY_CHUNK, w))
            v_rows = _row_block(_value_row(h_cur), V_SLOT)
            o_acc = jnp.zeros((V_SLOT, w), F32)
        for c in range(n_chunks):
            if nxt is not None:
                s = jnp.dot(k_ref[0, slab, rows(c), :], q_t, preferred_element_type=F32)
                s_next_ref[rows(c), :] = s
                folded = _fold_rows(s, jnp.maximum)
                m_next = folded if m_next is None else jnp.maximum(m_next, folded)
            if cur is not None:
                p = jnp.exp2(s_cur_ref[rows(c), :] - m_b)
                o_acc = o_acc + jnp.dot(v_ref[0, v_rows, rows(c)], p.astype(BF16),
                                        preferred_element_type=F32)
        if cur is not None:
            denom = o_acc[HEAD_DV:HEAD_DV + 1]
            ot_ref[0, _row_block(h_cur * HEAD_DV, HEAD_DV), cols(t_cur)] = (
                o_acc[:HEAD_DV] * (1.0 / denom)).astype(ot_ref.dtype)
        return m_next

    bufs = (s0_ref, s1_ref)

    def head(h, m, last=False):
        for t in range(n_parts):
            nxt = (h, t + 1) if t + 1 < n_parts else (None if last else (h + 1, 0))
            m = stage(nxt, bufs[(t + 1) % 2] if nxt is not None else None, (h, t), bufs[t % 2], m)
        return m

    m = lax.fori_loop(0, N_HEADS - 1, head, stage((0, 0), s0_ref, None, None, None))
    head(N_HEADS - 1, m, last=True)


def _attention(q_all, k_all, v_all, tq):
    nb, _, s, _ = k_all.shape
    w = min(QUERY_ITEM, tq // 2)
    assert s % KEY_CHUNK == 0 and w % LANES == 0 and tq % (2 * w) == 0
    score_buf = pltpu.VMEM((s, w), F32)
    return pl.pallas_call(
        _attention_kernel,
        grid=(nb, s // tq),
        in_specs=[pl.BlockSpec((1, N_HEADS * HEAD_SLOT, tq), lambda b, i: (b, 0, i)),
                  pl.BlockSpec((1, N_KEY_SLABS, s, HEAD_SLOT), lambda b, i: (b, 0, 0, 0)),
                  pl.BlockSpec((1, V_ROWS, s), lambda b, i: (b, 0, 0))],
        out_specs=pl.BlockSpec((1, D_MIX, tq), lambda b, i: (b, 0, i)),
        out_shape=jax.ShapeDtypeStruct((nb, D_MIX, s), BF16),
        scratch_shapes=[score_buf, score_buf],
        compiler_params=pltpu.CompilerParams(
            dimension_semantics=("arbitrary", "arbitrary"), vmem_limit_bytes=VMEM_LIMIT),
        name="attention",
    )(q_all, k_all, v_all)


def _attn_out_kernel(ot_ref, *refs, n_x, n_prompt):
    x_refs = refs[:n_x]
    mod_ref, go_ref, wo_ref, gffn_ref, x1_ref, h2_ref = refs[n_x:]
    o_t = ot_ref[0].astype(F32)
    oa = o_t[:WIDTH_A]
    ob = o_t[WIDTH_A:]
    ra = _rsqrt_mean(jnp.sum(oa * oa, axis=0, keepdims=True), WIDTH_A)
    rb = _rsqrt_mean(jnp.sum(ob * ob, axis=0, keepdims=True), WIDTH_B)
    on = jnp.concatenate([oa * ra, ob * rb], axis=0).T * go_ref[0]
    res = jnp.dot(on.astype(BF16), wo_ref[0], preferred_element_type=F32)
    mod = mod_ref[0]
    gate_a = mod[:, 2 * D_MODEL:3 * D_MODEL]
    shift_f = mod[:, 3 * D_MODEL:4 * D_MODEL]
    scale_f = mod[:, 4 * D_MODEL:5 * D_MODEL]
    x1 = _load_tokens(x_refs, n_prompt) + gate_a * res
    x1_ref[0] = x1
    r = _rsqrt_mean(jnp.sum(x1 * x1, axis=-1, keepdims=True), D_MODEL)
    h2_ref[0] = ((x1 * r) * gffn_ref[0] * (1.0 + scale_f) + shift_f).astype(BF16)


def _attn_out(o_t, xs, mod, l, w, tm):
    nb, _, s = o_t.shape
    lay = lambda *blk: pl.BlockSpec((1,) + blk, lambda b, i: (l,) + (0,) * len(blk))
    tok = pl.BlockSpec((1, tm, D_MODEL), lambda b, i: (b, i, 0))
    return pl.pallas_call(
        functools.partial(_attn_out_kernel, n_x=len(xs), n_prompt=xs[0].shape[0]),
        grid=(nb, s // tm),
        in_specs=[pl.BlockSpec((1, D_MIX, tm), lambda b, i: (b, 0, i))] + _token_specs(xs, tm, s // tm) + [
            pl.BlockSpec((1, 1, N_MOD * D_MODEL), lambda b, i: (b, 0, 0)),
            lay(1, D_MIX),
            lay(D_MIX, D_MODEL),
            lay(1, D_MODEL),
        ],
        out_specs=(tok, tok),
        out_shape=(jax.ShapeDtypeStruct((nb, s, D_MODEL), F32), jax.ShapeDtypeStruct((nb, s, D_MODEL), BF16)),
        compiler_params=pltpu.CompilerParams(
            dimension_semantics=("arbitrary", "arbitrary"), vmem_limit_bytes=VMEM_LIMIT),
        name="attn_out",
    )(o_t, *xs, mod, w["g_out"], w["w_o"], w["g_ffn"])


def _ffn_kernel(hm_ref, hp_ref, hn_ref, x1_ref, mod_ref, wup_ref, cw_ref, cb_ref, wdn_ref, gfin_ref,
                out_ref, hext_ref, ua_ref, ub_ref, *, final):
    acc_ref = out_ref.at[0]
    i = pl.program_id(1)
    n_i = pl.num_programs(1)
    tm = hm_ref.shape[1]

    keep_prev = jnp.where(i > 0, 1.0, 0.0).astype(F32)
    keep_next = jnp.where(i < n_i - 1, 1.0, 0.0).astype(F32)
    hext_ref[0:HALO, :] = (hp_ref[0].astype(F32) * keep_prev).astype(BF16)
    hext_ref[HALO:HALO + tm, :] = hm_ref[0]
    hext_ref[HALO + tm:HALO + tm + HALO, :] = (hn_ref[0].astype(F32) * keep_next).astype(BF16)
    acc_ref[...] = jnp.zeros((tm, D_MODEL), F32)

    def up(c, u_ref):
        u = jnp.dot(hext_ref[...], wup_ref[0, c], preferred_element_type=F32)
        for j in range(FF_SLABS):
            u_ref[j] = u[:, LANES * j:LANES * (j + 1)]

    def mix(c, u_ref):
        cw = cw_ref[0, c]
        cb = cb_ref[0, c]
        ys = []
        for j in range(FF_SLABS):
            lanes = slice(LANES * j, LANES * (j + 1))
            ys.append(u_ref[j, HALO - 1:HALO - 1 + tm, :] * cw[0:1, lanes]
                      + u_ref[j, HALO:HALO + tm, :] * cw[1:2, lanes]
                      + u_ref[j, HALO + 1:HALO + 1 + tm, :] * cw[2:3, lanes] + cb[:, lanes])
        half = FF_SLABS // 2
        acts = []
        for j in range(half):
            gate = ys[half + j]
            acts.append((gate * (1.0 / (1.0 + jnp.exp(-gate)))) * ys[j])
        act = jnp.concatenate(acts, axis=1).astype(BF16)
        acc_ref[...] += jnp.dot(act, wdn_ref[0, c], preferred_element_type=F32)

    def pair(k, carry):
        c = 2 * k
        up(c + 1, ub_ref)
        mix(c, ua_ref)
        up(c + 2, ua_ref)
        mix(c + 1, ub_ref)
        return carry

    assert N_FF_CHUNKS % 2 == 1
    up(0, ua_ref)
    lax.fori_loop(0, N_FF_CHUNKS // 2, pair, 0)
    mix(N_FF_CHUNKS - 1, ua_ref)

    gate_f = mod_ref[0][:, 5 * D_MODEL:6 * D_MODEL]
    x2 = x1_ref[0] + gate_f * acc_ref[...]
    if final:
        r = _rsqrt_mean(jnp.sum(x2 * x2, axis=-1, keepdims=True), D_MODEL)
        x2 = (x2 * r) * gfin_ref[...]
    out_ref[0] = x2


def _ffn(h2, x1, mod, l, w, g_final, tm, final, b0=0, nb=None):
    nb = x1.shape[0] if nb is None else nb
    s = x1.shape[1]
    n_i = s // tm
    hb = tm // HALO
    tok = pl.BlockSpec((1, tm, D_MODEL), lambda b, i: (b + b0, i, 0))
    resident = lambda *blk: pl.BlockSpec((1,) + blk, lambda b, i: (l,) + (0,) * len(blk),
                                         pipeline_mode=pl.Buffered(1))
    u_buf = pltpu.VMEM((FF_SLABS, tm + 2 * HALO, LANES), F32)
    return pl.pallas_call(
        functools.partial(_ffn_kernel, final=final),
        grid=(nb, n_i),
        in_specs=[
            tok,
            pl.BlockSpec((1, HALO, D_MODEL), lambda b, i: (b + b0, jnp.maximum(i * hb - 1, 0), 0)),
            pl.BlockSpec((1, HALO, D_MODEL),
                         lambda b, i: (b + b0, jnp.minimum((i + 1) * hb, s // HALO - 1), 0)),
            tok,
            pl.BlockSpec((1, 1, N_MOD * D_MODEL), lambda b, i: (b + b0, 0, 0)),
            resident(N_FF_CHUNKS, D_MODEL, 2 * FF_CHUNK),
            resident(N_FF_CHUNKS, 3, 2 * FF_CHUNK),
            resident(N_FF_CHUNKS, 1, 2 * FF_CHUNK),
            resident(N_FF_CHUNKS, FF_CHUNK, D_MODEL),
            pl.BlockSpec((1, D_MODEL), lambda b, i: (0, 0)),
        ],
        out_specs=pl.BlockSpec((1, tm, D_MODEL), lambda b, i: (b, i, 0)),
        out_shape=jax.ShapeDtypeStruct((nb, s, D_MODEL), F32),
        scratch_shapes=[pltpu.VMEM((tm + 2 * HALO, D_MODEL), BF16), u_buf, u_buf],
        compiler_params=pltpu.CompilerParams(
            dimension_semantics=("arbitrary", "arbitrary"), vmem_limit_bytes=VMEM_LIMIT),
        name="ffn",
    )(h2, h2, h2, x1, mod, w["w_up"], w["conv_w"], w["conv_b"], w["w_down"], g_final)


def _rope_tables_t(n_tokens, rot_dim):
    rows = n_tokens // GRID_W
    row = jnp.repeat(jnp.arange(rows, dtype=F32), GRID_W)
    col = jnp.tile(jnp.arange(GRID_W, dtype=F32), rows)
    quarter = rot_dim // 4
    inv_freq = ROPE_THETA ** (-jnp.arange(quarter, dtype=F32) / quarter)
    ang = jnp.stack([row, col], axis=-1)[:, :, None] * inv_freq
    cos, sin = jnp.cos(ang), jnp.sin(ang)
    c = jnp.concatenate([cos[:, 0], cos[:, 0], cos[:, 1], cos[:, 1]], axis=-1).T
    s = jnp.concatenate([-sin[:, 0], sin[:, 0], -sin[:, 1], sin[:, 1]], axis=-1).T
    return c, s


def _prepare_weights(g_attn, w_in, g_q_a, g_k_a, g_cq, w_uq, g_ckv, w_ukv, g_out_a, g_out_b, w_o,
                     g_ffn, w_up, conv_w, conv_b, w_down, tm):
    o_kr = D_IN - QK_ROPE_DIM
    w_in_p = jnp.concatenate([
        w_in[..., :o_kr], jnp.zeros((DEPTH, D_MODEL, ROPE_SLOT_OFF), F32), w_in[..., o_kr:],
        jnp.zeros((DEPTH, D_MODEL, LANES - ROPE_SLOT_OFF - QK_ROPE_DIM), F32)], axis=-1).astype(BF16)
    hq = QK_NOPE_DIM + QK_ROPE_DIM
    w_uq_p = jnp.pad(w_uq.reshape(DEPTH, Q_LORA, N_HEADS_B, hq), ((0, 0), (0, 0), (0, 0), (0, HEAD_SLOT - hq)))
    w_uq_t = w_uq_p.reshape(DEPTH, Q_LORA, N_HEADS_B * HEAD_SLOT).transpose(0, 2, 1).astype(BF16)
    w_ukv_h = w_ukv.reshape(DEPTH, KV_LORA, N_HEADS_B, QK_NOPE_DIM + V_DIM_B)
    w_uk = jnp.pad(w_ukv_h[..., :QK_NOPE_DIM], ((0, 0), (0, 0), (0, 0), (0, HEAD_SLOT - QK_NOPE_DIM)))
    w_uk = w_uk.reshape(DEPTH, KV_LORA, N_HEADS_B * HEAD_SLOT).astype(BF16)
    w_uv_t = w_ukv_h[..., QK_NOPE_DIM:].reshape(DEPTH, KV_LORA, WIDTH_B).transpose(0, 2, 1).astype(BF16)
    halves = lambda a: jnp.stack([a[..., :D_FF].reshape(a.shape[:-1] + (N_FF_CHUNKS, FF_CHUNK)),
                                  a[..., D_FF:].reshape(a.shape[:-1] + (N_FF_CHUNKS, FF_CHUNK))], axis=-2)
    w_up_c = halves(w_up).transpose(0, 2, 1, 3, 4).reshape(DEPTH, N_FF_CHUNKS, D_MODEL, 2 * FF_CHUNK).astype(BF16)
    conv_w_c = halves(conv_w).transpose(0, 2, 1, 3, 4).reshape(DEPTH, N_FF_CHUNKS, 3, 2 * FF_CHUNK)
    conv_b_c = halves(conv_b).reshape(DEPTH, N_FF_CHUNKS, 1, 2 * FF_CHUNK)
    w_down_c = w_down.reshape(DEPTH, N_FF_CHUNKS, FF_CHUNK, D_MODEL).astype(BF16)
    col = lambda g: jnp.broadcast_to(g[:, :, None], g.shape + (tm,))
    return {
        "g_attn": g_attn.reshape(DEPTH, 1, D_MODEL), "w_in": w_in_p,
        "gq_t": col(g_q_a), "gk_t": col(g_k_a), "gcq_t": col(g_cq), "w_uq_t": w_uq_t,
        "g_ckv": g_ckv.reshape(DEPTH, 1, KV_LORA), "w_uk": w_uk, "w_uv_t": w_uv_t,
        "g_out": jnp.concatenate([g_out_a, g_out_b], axis=-1).reshape(DEPTH, 1, D_MIX),
        "w_o": w_o.astype(BF16), "g_ffn": g_ffn.reshape(DEPTH, 1, D_MODEL),
        "w_up": w_up_c, "conv_w": conv_w_c, "conv_b": conv_b_c, "w_down": w_down_c,
    }


def _tiles(s):
    return min(1024, s), min(2048, s), min(1024, s)


def kernel(x_prompt, x_sample, c_prompt, c_sample, w_ada, b_ada, g_attn, w_in, g_q_a, g_k_a, g_cq, w_uq,
           g_ckv, w_ukv, g_out_a, g_out_b, w_o, g_ffn, w_up, conv_w, conv_b, w_down, g_final):
    n_prompt = x_prompt.shape[0]
    s = x_prompt.shape[1]
    assert x_sample.shape[1] == s and s % GRID_W == 0
    tm, tq, tf = _tiles(s)
    assert all(s % t == 0 for t in (tm, tq, tf)) and tf % HALO == 0
    xs = (x_prompt, x_sample)
    c = jnp.concatenate([c_prompt, c_sample], axis=0)
    nb = c.shape[0]

    w = _prepare_weights(g_attn, w_in, g_q_a, g_k_a, g_cq, w_uq, g_ckv, w_ukv, g_out_a, g_out_b, w_o,
                         g_ffn, w_up, conv_w, conv_b, w_down, tm)
    ca, sa = _rope_tables_t(s, HEAD_DIM_A)
    cb, sb = _rope_tables_t(s, QK_ROPE_DIM)
    tabs = {"ca": ca, "sa": sa, "cb": cb, "sb": sb}
    mod_all = _modulation(c, w_ada, b_ada)
    g_fin = g_final.reshape(1, D_MODEL)

    for l in range(DEPTH):
        mod = mod_all[l].reshape(nb, 1, N_MOD * D_MODEL)
        q_all, k_all, v_all = _attn_in(xs, mod, l, w, tabs, tm)
        o_t = _attention(q_all, k_all, v_all, tq)
        x1, h2 = _attn_out(o_t, xs, mod, l, w, tm)
        if l < DEPTH - 1:
            xs = (_ffn(h2, x1, mod, l, w, g_fin, tf, final=False),)
    return (_ffn(h2, x1, mod, DEPTH - 1, w, g_fin, tf, final=True, b0=0, nb=n_prompt),
            _ffn(h2, x1, mod, DEPTH - 1, w, g_fin, tf, final=True, b0=n_prompt, nb=nb - n_prompt))
```

```python
import functools
import math

import jax
import jax.numpy as jnp
from jax import lax
from jax.experimental import pallas as pl
from jax.experimental.pallas import tpu as pltpu

F32 = jnp.float32
BF16 = jnp.bfloat16

D_MODEL = 1024
DEPTH = 4
GRID_W = 64
ROPE_THETA = 10000.0
EPS = 1e-6
N_HEADS_A = 8
N_KV_HEADS_A = 2
HEAD_DIM_A = 64
N_HEADS_B = 8
QK_NOPE_DIM = 64
QK_ROPE_DIM = 32
V_DIM_B = 64
Q_LORA = 256
KV_LORA = 128
WIDTH_A = N_HEADS_A * HEAD_DIM_A
WIDTH_B = N_HEADS_B * V_DIM_B
D_MIX = WIDTH_A + WIDTH_B
KV_WIDTH_A = N_KV_HEADS_A * HEAD_DIM_A
D_IN = WIDTH_A + 2 * KV_WIDTH_A + Q_LORA + KV_LORA + QK_ROPE_DIM
D_FF = 2816
N_MOD = 6

LANES = 128
HEAD_SLOT = 128
D_IN_PAD = 1280
ROPE_SLOT_OFF = 64
FF_CHUNK = 256
N_FF_CHUNKS = D_FF // FF_CHUNK
FF_SLABS = 2 * FF_CHUNK // LANES
HALO = 16
N_HEADS = N_HEADS_A + N_HEADS_B
N_KEY_SLABS = 1 + N_HEADS_B
HEAD_DV = HEAD_DIM_A
assert HEAD_DV == V_DIM_B
V_SLOT = HEAD_DV + 16
V_ROWS = (N_KV_HEADS_A + N_HEADS_B) * V_SLOT
QUERY_ITEM = 512
KEY_CHUNK = 256
FOLD_ROWS = 8
SUB_ROWS = 256
LOG2E = math.log2(math.e)
VMEM_LIMIT = 56 * 1024 * 1024


def _rsqrt_mean(ss, n):
    return lax.rsqrt(ss * (1.0 / n) + EPS)


def _mod_kernel(c_ref, w_ref, b_ref, o_ref):
    c = c_ref[...]
    ca = c * (1.0 / (1.0 + jnp.exp(-c)))
    o_ref[0] = jnp.dot(ca.astype(BF16), w_ref[0].astype(BF16), preferred_element_type=F32) + b_ref[0]


def _modulation(c, w_ada, b_ada):
    nb = c.shape[0]
    n_col = N_MOD * D_MODEL
    tn = D_MODEL
    return pl.pallas_call(
        _mod_kernel,
        grid=(DEPTH, n_col // tn),
        in_specs=[
            pl.BlockSpec((nb, D_MODEL), lambda l, j: (0, 0)),
            pl.BlockSpec((1, D_MODEL, tn), lambda l, j: (l, 0, j)),
            pl.BlockSpec((1, 1, tn), lambda l, j: (l, 0, j)),
        ],
        out_specs=pl.BlockSpec((1, nb, tn), lambda l, j: (l, 0, j)),
        out_shape=jax.ShapeDtypeStruct((DEPTH, nb, n_col), F32),
        compiler_params=pltpu.CompilerParams(dimension_semantics=("arbitrary", "arbitrary")),
        name="modulation",
    )(c, w_ada, b_ada.reshape(DEPTH, 1, n_col))


def _rope_t(x, c, s, q):
    sw = jnp.concatenate([x[q:2 * q], x[0:q], x[3 * q:4 * q], x[2 * q:3 * q]], axis=0)
    return x * c + sw * s


def _token_specs(xs, tm, n_i):
    if len(xs) == 1:
        return [pl.BlockSpec((1, tm, D_MODEL), lambda b, i: (b, i, 0))]
    n_p = xs[0].shape[0]
    return [pl.BlockSpec((1, tm, D_MODEL),
                         lambda b, i: (jnp.minimum(b, n_p - 1), jnp.where(b < n_p, i, n_i - 1), 0)),
            pl.BlockSpec((1, tm, D_MODEL),
                         lambda b, i: (jnp.maximum(b - n_p, 0), jnp.where(b >= n_p, i, 0), 0))]


def _load_tokens(x_refs, n_prompt):
    if len(x_refs) == 1:
        return x_refs[0][0]
    return jnp.where(pl.program_id(0) < n_prompt, x_refs[0][0], x_refs[1][0])


def _attn_in_kernel(*refs, n_x, n_prompt):
    x_refs = refs[:n_x]
    (mod_ref, gat_ref, win_ref, gq_ref, gk_ref, gcq_ref, wuqt_ref, gckv_ref, wuk_ref, wuvt_ref,
     ca_ref, sa_ref, cb_ref, sb_ref, q_ref, k_ref, v_ref) = refs[n_x:]
    x_all = _load_tokens(x_refs, n_prompt)
    tm_all = x_all.shape[0]
    tm = min(SUB_ROWS, tm_all)
    mod = mod_ref[0]
    shift = mod[:, 0:D_MODEL]
    scale = mod[:, D_MODEL:2 * D_MODEL]

    def project(t):
        x = x_all[tm * t:tm * (t + 1)]
        r = _rsqrt_mean(jnp.sum(x * x, axis=-1, keepdims=True), D_MODEL)
        h = (x * r) * gat_ref[0] * (1.0 + scale) + shift
        return jnp.dot(h.astype(BF16), win_ref[0], preferred_element_type=F32)

    def finish(t, proj):
        tok = slice(tm * t, tm * (t + 1))
        ca = ca_ref[:, tok]
        sa = sa_ref[:, tok]
        cb = cb_ref[:, tok]
        sb = sb_ref[:, tok]
        zeros_half = jnp.zeros((HEAD_DIM_A, tm), F32)

        o_qa = 0
        q_at = proj[:, o_qa:o_qa + WIDTH_A].T
        qscale_a = (HEAD_DIM_A ** -0.5) * LOG2E
        gq = gq_ref[0, :, tok]
        for j in range(N_HEADS_A):
            xh = q_at[HEAD_DIM_A * j:HEAD_DIM_A * (j + 1)]
            rh = _rsqrt_mean(jnp.sum(xh * xh, axis=0, keepdims=True), HEAD_DIM_A)
            xr = _rope_t((xh * rh) * gq, ca, sa, HEAD_DIM_A // 4) * qscale_a
            g = j // (N_HEADS_A // N_KV_HEADS_A)
            slot = jnp.concatenate([xr, zeros_half] if g == 0 else [zeros_half, xr], axis=0)
            q_ref[0, HEAD_SLOT * j:HEAD_SLOT * (j + 1), tok] = slot.astype(BF16)

        o_ka = WIDTH_A
        k_at = proj[:, o_ka:o_ka + KV_WIDTH_A].T
        gk = gk_ref[0, :, tok]
        kparts = []
        for g in range(N_KV_HEADS_A):
            xh = k_at[HEAD_DIM_A * g:HEAD_DIM_A * (g + 1)]
            rh = _rsqrt_mean(jnp.sum(xh * xh, axis=0, keepdims=True), HEAD_DIM_A)
            kparts.append(_rope_t((xh * rh) * gk, ca, sa, HEAD_DIM_A // 4))
        k_ref[0, 0, tok, :] = jnp.concatenate(kparts, axis=0).T.astype(BF16)
        o_va = o_ka + KV_WIDTH_A
        ones_rows = jnp.where(
            lax.broadcasted_iota(jnp.int32, (V_SLOT - HEAD_DV, tm), 0) == 0, 1.0, 0.0).astype(BF16)

        def store_values(slot, v_t):
            v_ref[0, V_SLOT * slot:V_SLOT * slot + HEAD_DV, tok] = v_t.astype(BF16)
            v_ref[0, V_SLOT * slot + HEAD_DV:V_SLOT * (slot + 1), tok] = ones_rows

        va_t = proj[:, o_va:o_va + KV_WIDTH_A].T
        for g in range(N_KV_HEADS_A):
            store_values(g, va_t[HEAD_DV * g:HEAD_DV * (g + 1)])

        o_cq = o_va + KV_WIDTH_A
        cq_t = proj[:, o_cq:o_cq + Q_LORA].T
        rq = _rsqrt_mean(jnp.sum(cq_t * cq_t, axis=0, keepdims=True), Q_LORA)
        cqn = (cq_t * rq) * gcq_ref[0, :, tok]
        qb_t = jnp.dot(wuqt_ref[0], cqn.astype(BF16), preferred_element_type=F32)
        qscale_b = ((QK_NOPE_DIM + QK_ROPE_DIM) ** -0.5) * LOG2E
        zeros_pad = jnp.zeros((HEAD_SLOT - QK_NOPE_DIM - QK_ROPE_DIM, tm), F32)
        for j in range(N_HEADS_B):
            base = HEAD_SLOT * j
            nope = qb_t[base:base + QK_NOPE_DIM]
            rope = _rope_t(qb_t[base + ROPE_SLOT_OFF:base + ROPE_SLOT_OFF + QK_ROPE_DIM], cb, sb,
                           QK_ROPE_DIM // 4)
            slot = jnp.concatenate([nope, rope, zeros_pad], axis=0) * qscale_b
            row0 = N_HEADS_A * HEAD_SLOT + base
            q_ref[0, row0:row0 + HEAD_SLOT, tok] = slot.astype(BF16)

        o_ckv = o_cq + Q_LORA
        ckv = proj[:, o_ckv:o_ckv + KV_LORA]
        rkv = _rsqrt_mean(jnp.sum(ckv * ckv, axis=-1, keepdims=True), KV_LORA)
        ckvn = (ckv * rkv) * gckv_ref[0]
        k_nope = jnp.dot(ckvn.astype(BF16), wuk_ref[0], preferred_element_type=F32)
        vb_t = jnp.dot(wuvt_ref[0], ckvn.T.astype(BF16), preferred_element_type=F32)
        for j in range(N_HEADS_B):
            store_values(N_KV_HEADS_A + j, vb_t[HEAD_DV * j:HEAD_DV * (j + 1)])
        o_kr = o_ckv + KV_LORA
        kr_t = proj[:, o_kr:o_kr + LANES].T
        kr_rope = _rope_t(kr_t[ROPE_SLOT_OFF:ROPE_SLOT_OFF + QK_ROPE_DIM], cb, sb, QK_ROPE_DIM // 4)
        kr_tile = jnp.concatenate(
            [jnp.zeros((ROPE_SLOT_OFF, tm), F32), kr_rope, zeros_pad], axis=0).T
        for j in range(N_HEADS_B):
            base = HEAD_SLOT * j
            k_ref[0, 1 + j, tok, :] = (k_nope[:, base:base + HEAD_SLOT] + kr_tile).astype(BF16)

    n_sub = tm_all // tm
    projs = [project(t) for t in range(n_sub)]
    for t in range(n_sub):
        finish(t, projs[t])


def _attn_in(xs, mod, l, w, tabs, tm):
    nb = sum(x.shape[0] for x in xs)
    s = xs[0].shape[1]
    grid = (nb, s // tm)
    lay = lambda *blk: pl.BlockSpec((1,) + blk, lambda b, i: (l,) + (0,) * len(blk))
    tab = lambda rows: pl.BlockSpec((rows, tm), lambda b, i: (0, i))
    out_shapes = (
        jax.ShapeDtypeStruct((nb, N_HEADS * HEAD_SLOT, s), BF16),
        jax.ShapeDtypeStruct((nb, N_KEY_SLABS, s, HEAD_SLOT), BF16),
        jax.ShapeDtypeStruct((nb, V_ROWS, s), BF16),
    )
    return pl.pallas_call(
        functools.partial(_attn_in_kernel, n_x=len(xs), n_prompt=xs[0].shape[0]),
        grid=grid,
        in_specs=_token_specs(xs, tm, s // tm) + [
            pl.BlockSpec((1, 1, N_MOD * D_MODEL), lambda b, i: (b, 0, 0)),
            lay(1, D_MODEL),
            lay(D_MODEL, D_IN_PAD),
            lay(HEAD_DIM_A, tm),
            lay(HEAD_DIM_A, tm),
            lay(Q_LORA, tm),
            lay(N_HEADS_B * HEAD_SLOT, Q_LORA),
            lay(1, KV_LORA),
            lay(KV_LORA, N_HEADS_B * HEAD_SLOT),
            lay(WIDTH_B, KV_LORA),
            tab(HEAD_DIM_A), tab(HEAD_DIM_A), tab(QK_ROPE_DIM), tab(QK_ROPE_DIM),
        ],
        out_specs=(pl.BlockSpec((1, N_HEADS * HEAD_SLOT, tm), lambda b, i: (b, 0, i)),
                   pl.BlockSpec((1, N_KEY_SLABS, tm, HEAD_SLOT), lambda b, i: (b, 0, i, 0)),
                   pl.BlockSpec((1, V_ROWS, tm), lambda b, i: (b, 0, i))),
        out_shape=out_shapes,
        compiler_params=pltpu.CompilerParams(
            dimension_semantics=("arbitrary", "arbitrary"), vmem_limit_bytes=VMEM_LIMIT),
        name="attn_in",
    )(*xs, mod, w["g_attn"], w["w_in"], w["gq_t"], w["gk_t"], w["gcq_t"], w["w_uq_t"], w["g_ckv"],
      w["w_uk"], w["w_uv_t"], tabs["ca"], tabs["sa"], tabs["cb"], tabs["sb"])


def _fold_rows(x, op):
    parts = [x[FOLD_ROWS * i:FOLD_ROWS * (i + 1)] for i in range(x.shape[0] // FOLD_ROWS)]
    while len(parts) > 1:
        parts = [op(parts[2 * i], parts[2 * i + 1]) for i in range(len(parts) // 2)]
    return parts[0]


def _key_slab(h):
    if isinstance(h, int):
        return 0 if h < N_HEADS_A else h - N_HEADS_A + 1
    return jnp.where(h < N_HEADS_A, 0, h - N_HEADS_A + 1)


def _value_row(h):
    per_kv = N_HEADS_A // N_KV_HEADS_A
    if isinstance(h, int):
        return V_SLOT * (h // per_kv if h < N_HEADS_A else h - N_HEADS_A + N_KV_HEADS_A)
    return V_SLOT * jnp.where(h < N_HEADS_A, h // per_kv, h - N_HEADS_A + N_KV_HEADS_A)


def _row_block(start, size):
    if isinstance(start, int):
        return pl.ds(start, size)
    return pl.ds(pl.multiple_of(start, size), size)


def _attention_kernel(q_ref, k_ref, v_ref, ot_ref, s0_ref, s1_ref):
    n_keys = k_ref.shape[2]
    w = s0_ref.shape[1]
    n_parts = ot_ref.shape[2] // w
    n_chunks = n_keys // KEY_CHUNK
    rows = lambda c: slice(KEY_CHUNK * c, KEY_CHUNK * (c + 1))
    cols = lambda t: slice(w * t, w * (t + 1))

    def stage(nxt, s_next_ref, cur, s_cur_ref, m_cur):
        m_next = None
        if nxt is not None:
            h_next, t_next = nxt
            q_t = q_ref[0, _row_block(h_next * HEAD_SLOT, HEAD_SLOT), cols(t_next)]
            slab = _key_slab(h_next)
        if cur is not None:
            h_cur, t_cur = cur
            m_b = jnp.broadcast_to(jnp.max(m_cur, axis=0, keepdims=True), (KEY_CHUNK, w))
            v_rows = _row_block(_value_row(h_cur), V_SLOT)
            o_acc = jnp.zeros((V_SLOT, w), F32)
        for c in range(n_chunks):
            if nxt is not None:
                s = jnp.dot(k_ref[0, slab, rows(c), :], q_t, preferred_element_type=F32)
                s_next_ref[rows(c), :] = s
                folded = _fold_rows(s, jnp.maximum)
                m_next = folded if m_next is None else jnp.maximum(m_next, folded)
            if cur is not None:
                p = jnp.exp2(s_cur_ref[rows(c), :] - m_b)
                o_acc = o_acc + jnp.dot(v_ref[0, v_rows, rows(c)], p.astype(BF16),
                                        preferred_element_type=F32)
        if cur is not None:
            denom = o_acc[HEAD_DV:HEAD_DV + 1]
            ot_ref[0, _row_block(h_cur * HEAD_DV, HEAD_DV), cols(t_cur)] = (
                o_acc[:HEAD_DV] * (1.0 / denom)).astype(ot_ref.dtype)
        return m_next

    bufs = (s0_ref, s1_ref)

    def head(h, m, last=False):
        for t in range(n_parts):
            nxt = (h, t + 1) if t + 1 < n_parts else (None if last else (h + 1, 0))
            m = stage(nxt, bufs[(t + 1) % 2] if nxt is not None else None, (h, t), bufs[t % 2], m)
        return m

    m = lax.fori_loop(0, N_HEADS - 1, head, stage((0, 0), s0_ref, None, None, None))
    head(N_HEADS - 1, m, last=True)


def _attention(q_all, k_all, v_all, tq):
    nb, _, s, _ = k_all.shape
    w = min(QUERY_ITEM, tq // 2)
    assert s % KEY_CHUNK == 0 and w % LANES == 0 and tq % (2 * w) == 0
    score_buf = pltpu.VMEM((s, w), F32)
    return pl.pallas_call(
        _attention_kernel,
        grid=(nb, s // tq),
        in_specs=[pl.BlockSpec((1, N_HEADS * HEAD_SLOT, tq), lambda b, i: (b, 0, i)),
                  pl.BlockSpec((1, N_KEY_SLABS, s, HEAD_SLOT), lambda b, i: (b, 0, 0, 0)),
                  pl.BlockSpec((1, V_ROWS, s), lambda b, i: (b, 0, 0))],
        out_specs=pl.BlockSpec((1, D_MIX, tq), lambda b, i: (b, 0, i)),
        out_shape=jax.ShapeDtypeStruct((nb, D_MIX, s), BF16),
        scratch_shapes=[score_buf, score_buf],
        compiler_params=pltpu.CompilerParams(
            dimension_semantics=("arbitrary", "arbitrary"), vmem_limit_bytes=VMEM_LIMIT),
        name="attention",
    )(q_all, k_all, v_all)


def _attn_out_kernel(ot_ref, *refs, n_x, n_prompt):
    x_refs = refs[:n_x]
    mod_ref, go_ref, wo_ref, gffn_ref, x1_ref, h2_ref = refs[n_x:]
    o_t = ot_ref[0].astype(F32)
    oa = o_t[:WIDTH_A]
    ob = o_t[WIDTH_A:]
    ra = _rsqrt_mean(jnp.sum(oa * oa, axis=0, keepdims=True), WIDTH_A)
    rb = _rsqrt_mean(jnp.sum(ob * ob, axis=0, keepdims=True), WIDTH_B)
    on = jnp.concatenate([oa * ra, ob * rb], axis=0).T * go_ref[0]
    res = jnp.dot(on.astype(BF16), wo_ref[0], preferred_element_type=F32)
    mod = mod_ref[0]
    gate_a = mod[:, 2 * D_MODEL:3 * D_MODEL]
    shift_f = mod[:, 3 * D_MODEL:4 * D_MODEL]
    scale_f = mod[:, 4 * D_MODEL:5 * D_MODEL]
    x1 = _load_tokens(x_refs, n_prompt) + gate_a * res
    x1_ref[0] = x1
    r = _rsqrt_mean(jnp.sum(x1 * x1, axis=-1, keepdims=True), D_MODEL)
    h2_ref[0] = ((x1 * r) * gffn_ref[0] * (1.0 + scale_f) + shift_f).astype(BF16)


def _attn_out(o_t, xs, mod, l, w, tm):
    nb, _, s = o_t.shape
    lay = lambda *blk: pl.BlockSpec((1,) + blk, lambda b, i: (l,) + (0,) * len(blk))
    tok = pl.BlockSpec((1, tm, D_MODEL), lambda b, i: (b, i, 0))
    return pl.pallas_call(
        functools.partial(_attn_out_kernel, n_x=len(xs), n_prompt=xs[0].shape[0]),
        grid=(nb, s // tm),
        in_specs=[pl.BlockSpec((1, D_MIX, tm), lambda b, i: (b, 0, i))] + _token_specs(xs, tm, s // tm) + [
            pl.BlockSpec((1, 1, N_MOD * D_MODEL), lambda b, i: (b, 0, 0)),
            lay(1, D_MIX),
            lay(D_MIX, D_MODEL),
            lay(1, D_MODEL),
        ],
        out_specs=(tok, tok),
        out_shape=(jax.ShapeDtypeStruct((nb, s, D_MODEL), F32), jax.ShapeDtypeStruct((nb, s, D_MODEL), BF16)),
        compiler_params=pltpu.CompilerParams(
            dimension_semantics=("arbitrary", "arbitrary"), vmem_limit_bytes=VMEM_LIMIT),
        name="attn_out",
    )(o_t, *xs, mod, w["g_out"], w["w_o"], w["g_ffn"])


def _ffn_kernel(hm_ref, hp_ref, hn_ref, x1_ref, mod_ref, wup_ref, cw_ref, cb_ref, wdn_ref, gfin_ref,
                out_ref, hext_ref, ua_ref, ub_ref, *, final):
    acc_ref = out_ref.at[0]
    i = pl.program_id(1)
    n_i = pl.num_programs(1)
    tm = hm_ref.shape[1]

    keep_prev = jnp.where(i > 0, 1.0, 0.0).astype(F32)
    keep_next = jnp.where(i < n_i - 1, 1.0, 0.0).astype(F32)
    hext_ref[0:HALO, :] = (hp_ref[0].astype(F32) * keep_prev).astype(BF16)
    hext_ref[HALO:HALO + tm, :] = hm_ref[0]
    hext_ref[HALO + tm:HALO + tm + HALO, :] = (hn_ref[0].astype(F32) * keep_next).astype(BF16)
    acc_ref[...] = jnp.zeros((tm, D_MODEL), F32)

    def up(c, u_ref):
        u = jnp.dot(hext_ref[...], wup_ref[0, c], preferred_element_type=F32)
        for j in range(FF_SLABS):
            u_ref[j] = u[:, LANES * j:LANES * (j + 1)]

    def mix(c, u_ref):
        cw = cw_ref[0, c]
        cb = cb_ref[0, c]
        ys = []
        for j in range(FF_SLABS):
            lanes = slice(LANES * j, LANES * (j + 1))
            ys.append(u_ref[j, HALO - 1:HALO - 1 + tm, :] * cw[0:1, lanes]
                      + u_ref[j, HALO:HALO + tm, :] * cw[1:2, lanes]
                      + u_ref[j, HALO + 1:HALO + 1 + tm, :] * cw[2:3, lanes] + cb[:, lanes])
        half = FF_SLABS // 2
        acts = []
        for j in range(half):
            gate = ys[half + j]
            acts.append((gate * (1.0 / (1.0 + jnp.exp(-gate)))) * ys[j])
        act = jnp.concatenate(acts, axis=1).astype(BF16)
        acc_ref[...] += jnp.dot(act, wdn_ref[0, c], preferred_element_type=F32)

    def pair(k, carry):
        c = 2 * k
        up(c + 1, ub_ref)
        mix(c, ua_ref)
        up(c + 2, ua_ref)
        mix(c + 1, ub_ref)
        return carry

    assert N_FF_CHUNKS % 2 == 1
    up(0, ua_ref)
    lax.fori_loop(0, N_FF_CHUNKS // 2, pair, 0)
    mix(N_FF_CHUNKS - 1, ua_ref)

    gate_f = mod_ref[0][:, 5 * D_MODEL:6 * D_MODEL]
    x2 = x1_ref[0] + gate_f * acc_ref[...]
    if final:
        r = _rsqrt_mean(jnp.sum(x2 * x2, axis=-1, keepdims=True), D_MODEL)
        x2 = (x2 * r) * gfin_ref[...]
    out_ref[0] = x2


def _ffn(h2, x1, mod, l, w, g_final, tm, final, b0=0, nb=None):
    nb = x1.shape[0] if nb is None else nb
    s = x1.shape[1]
    n_i = s // tm
    hb = tm // HALO
    tok = pl.BlockSpec((1, tm, D_MODEL), lambda b, i: (b + b0, i, 0))
    resident = lambda *blk: pl.BlockSpec((1,) + blk, lambda b, i: (l,) + (0,) * len(blk),
                                         pipeline_mode=pl.Buffered(1))
    u_buf = pltpu.VMEM((FF_SLABS, tm + 2 * HALO, LANES), F32)
    return pl.pallas_call(
        functools.partial(_ffn_kernel, final=final),
        grid=(nb, n_i),
        in_specs=[
            tok,
            pl.BlockSpec((1, HALO, D_MODEL), lambda b, i: (b + b0, jnp.maximum(i * hb - 1, 0), 0)),
            pl.BlockSpec((1, HALO, D_MODEL),
                         lambda b, i: (b + b0, jnp.minimum((i + 1) * hb, s // HALO - 1), 0)),
            tok,
            pl.BlockSpec((1, 1, N_MOD * D_MODEL), lambda b, i: (b + b0, 0, 0)),
            resident(N_FF_CHUNKS, D_MODEL, 2 * FF_CHUNK),
            resident(N_FF_CHUNKS, 3, 2 * FF_CHUNK),
            resident(N_FF_CHUNKS, 1, 2 * FF_CHUNK),
            resident(N_FF_CHUNKS, FF_CHUNK, D_MODEL),
            pl.BlockSpec((1, D_MODEL), lambda b, i: (0, 0)),
        ],
        out_specs=pl.BlockSpec((1, tm, D_MODEL), lambda b, i: (b, i, 0)),
        out_shape=jax.ShapeDtypeStruct((nb, s, D_MODEL), F32),
        scratch_shapes=[pltpu.VMEM((tm + 2 * HALO, D_MODEL), BF16), u_buf, u_buf],
        compiler_params=pltpu.CompilerParams(
            dimension_semantics=("arbitrary", "arbitrary"), vmem_limit_bytes=VMEM_LIMIT),
        name="ffn",
    )(h2, h2, h2, x1, mod, w["w_up"], w["conv_w"], w["conv_b"], w["w_down"], g_final)


def _rope_tables_t(n_tokens, rot_dim):
    rows = n_tokens // GRID_W
    row = jnp.repeat(jnp.arange(rows, dtype=F32), GRID_W)
    col = jnp.tile(jnp.arange(GRID_W, dtype=F32), rows)
    quarter = rot_dim // 4
    inv_freq = ROPE_THETA ** (-jnp.arange(quarter, dtype=F32) / quarter)
    ang = jnp.stack([row, col], axis=-1)[:, :, None] * inv_freq
    cos, sin = jnp.cos(ang), jnp.sin(ang)
    c = jnp.concatenate([cos[:, 0], cos[:, 0], cos[:, 1], cos[:, 1]], axis=-1).T
    s = jnp.concatenate([-sin[:, 0], sin[:, 0], -sin[:, 1], sin[:, 1]], axis=-1).T
    return c, s


def _prepare_weights(g_attn, w_in, g_q_a, g_k_a, g_cq, w_uq, g_ckv, w_ukv, g_out_a, g_out_b, w_o,
                     g_ffn, w_up, conv_w, conv_b, w_down, tm):
    o_kr = D_IN - QK_ROPE_DIM
    w_in_p = jnp.concatenate([
        w_in[..., :o_kr], jnp.zeros((DEPTH, D_MODEL, ROPE_SLOT_OFF), F32), w_in[..., o_kr:],
        jnp.zeros((DEPTH, D_MODEL, LANES - ROPE_SLOT_OFF - QK_ROPE_DIM), F32)], axis=-1).astype(BF16)
    hq = QK_NOPE_DIM + QK_ROPE_DIM
    w_uq_p = jnp.pad(w_uq.reshape(DEPTH, Q_LORA, N_HEADS_B, hq), ((0, 0), (0, 0), (0, 0), (0, HEAD_SLOT - hq)))
    w_uq_t = w_uq_p.reshape(DEPTH, Q_LORA, N_HEADS_B * HEAD_SLOT).transpose(0, 2, 1).astype(BF16)
    w_ukv_h = w_ukv.reshape(DEPTH, KV_LORA, N_HEADS_B, QK_NOPE_DIM + V_DIM_B)
    w_uk = jnp.pad(w_ukv_h[..., :QK_NOPE_DIM], ((0, 0), (0, 0), (0, 0), (0, HEAD_SLOT - QK_NOPE_DIM)))
    w_uk = w_uk.reshape(DEPTH, KV_LORA, N_HEADS_B * HEAD_SLOT).astype(BF16)
    w_uv_t = w_ukv_h[..., QK_NOPE_DIM:].reshape(DEPTH, KV_LORA, WIDTH_B).transpose(0, 2, 1).astype(BF16)
    def chunked(a):
        cols = lambda c, off: a[..., off + FF_CHUNK * c:off + FF_CHUNK * (c + 1)]
        return jnp.stack([jnp.concatenate([cols(c, 0), cols(c, D_FF)], axis=-1) for c in range(N_FF_CHUNKS)],
                         axis=1)

    w_up_c = chunked(w_up.astype(BF16))
    conv_w_c = chunked(conv_w)
    conv_b_c = chunked(conv_b[:, None, :])
    w_down_c = w_down.reshape(DEPTH, N_FF_CHUNKS, FF_CHUNK, D_MODEL).astype(BF16)
    col = lambda g: jnp.broadcast_to(g[:, :, None], g.shape + (tm,))
    return {
        "g_attn": g_attn.reshape(DEPTH, 1, D_MODEL), "w_in": w_in_p,
        "gq_t": col(g_q_a), "gk_t": col(g_k_a), "gcq_t": col(g_cq), "w_uq_t": w_uq_t,
        "g_ckv": g_ckv.reshape(DEPTH, 1, KV_LORA), "w_uk": w_uk, "w_uv_t": w_uv_t,
        "g_out": jnp.concatenate([g_out_a, g_out_b], axis=-1).reshape(DEPTH, 1, D_MIX),
        "w_o": w_o.astype(BF16), "g_ffn": g_ffn.reshape(DEPTH, 1, D_MODEL),
        "w_up": w_up_c, "conv_w": conv_w_c, "conv_b": conv_b_c, "w_down": w_down_c,
    }


def _tiles(s):
    return min(1024, s), min(2048, s), min(1024, s)


def kernel(x_prompt, x_sample, c_prompt, c_sample, w_ada, b_ada, g_attn, w_in, g_q_a, g_k_a, g_cq, w_uq,
           g_ckv, w_ukv, g_out_a, g_out_b, w_o, g_ffn, w_up, conv_w, conv_b, w_down, g_final):
    n_prompt = x_prompt.shape[0]
    s = x_prompt.shape[1]
    assert x_sample.shape[1] == s and s % GRID_W == 0
    tm, tq, tf = _tiles(s)
    assert all(s % t == 0 for t in (tm, tq, tf)) and tf % HALO == 0
    xs = (x_prompt, x_sample)
    c = jnp.concatenate([c_prompt, c_sample], axis=0)
    nb = c.shape[0]

    w = _prepare_weights(g_attn, w_in, g_q_a, g_k_a, g_cq, w_uq, g_ckv, w_ukv, g_out_a, g_out_b, w_o,
                         g_ffn, w_up, conv_w, conv_b, w_down, tm)
    ca, sa = _rope_tables_t(s, HEAD_DIM_A)
    cb, sb = _rope_tables_t(s, QK_ROPE_DIM)
    tabs = {"ca": ca, "sa": sa, "cb": cb, "sb": sb}
    mod_all = _modulation(c, w_ada, b_ada)
    g_fin = g_final.reshape(1, D_MODEL)

    for l in range(DEPTH):
        mod = mod_all[l].reshape(nb, 1, N_MOD * D_MODEL)
        q_all, k_all, v_all = _attn_in(xs, mod, l, w, tabs, tm)
        o_t = _attention(q_all, k_all, v_all, tq)
        x1, h2 = _attn_out(o_t, xs, mod, l, w, tm)
        if l < DEPTH - 1:
            xs = (_ffn(h2, x1, mod, l, w, g_fin, tf, final=False),)
    return (_ffn(h2, x1, mod, DEPTH - 1, w, g_fin, tf, final=True, b0=0, nb=n_prompt),
            _ffn(h2, x1, mod, DEPTH - 1, w, g_fin, tf, final=True, b0=n_prompt, nb=nb - n_prompt))
```

```python
import functools
import math

import jax
import jax.numpy as jnp
from jax import lax
from jax.experimental import pallas as pl
from jax.experimental.pallas import tpu as pltpu

F32 = jnp.float32
BF16 = jnp.bfloat16

D_MODEL = 1024
DEPTH = 4
GRID_W = 64
ROPE_THETA = 10000.0
EPS = 1e-6
N_HEADS_A = 8
N_KV_HEADS_A = 2
HEAD_DIM_A = 64
N_HEADS_B = 8
QK_NOPE_DIM = 64
QK_ROPE_DIM = 32
V_DIM_B = 64
Q_LORA = 256
KV_LORA = 128
WIDTH_A = N_HEADS_A * HEAD_DIM_A
WIDTH_B = N_HEADS_B * V_DIM_B
D_MIX = WIDTH_A + WIDTH_B
KV_WIDTH_A = N_KV_HEADS_A * HEAD_DIM_A
D_IN = WIDTH_A + 2 * KV_WIDTH_A + Q_LORA + KV_LORA + QK_ROPE_DIM
D_FF = 2816
N_MOD = 6

LANES = 128
HEAD_SLOT = 128
D_IN_PAD = 1280
ROPE_SLOT_OFF = 64
FF_CHUNK = 256
N_FF_CHUNKS = D_FF // FF_CHUNK
FF_SLABS = 2 * FF_CHUNK // LANES
HALO = 16
N_HEADS = N_HEADS_A + N_HEADS_B
N_KEY_SLABS = 1 + N_HEADS_B
HEAD_DV = HEAD_DIM_A
assert HEAD_DV == V_DIM_B
V_SLOT = HEAD_DV + 16
V_ROWS = (N_KV_HEADS_A + N_HEADS_B) * V_SLOT
QUERY_ITEM = 512
KEY_CHUNK = 256
SAFE_LOG2 = 40
FOLD_ROWS = 8
SUB_ROWS = 256
LOG2E = math.log2(math.e)
VMEM_LIMIT = 56 * 1024 * 1024


def _rsqrt_mean(ss, n):
    return lax.rsqrt(ss * (1.0 / n) + EPS)


def _mod_kernel(c_ref, w_ref, b_ref, o_ref):
    c = c_ref[...]
    ca = c * (1.0 / (1.0 + jnp.exp(-c)))
    o_ref[0] = jnp.dot(ca.astype(BF16), w_ref[0].astype(BF16), preferred_element_type=F32) + b_ref[0]


def _modulation(c, w_ada, b_ada):
    nb = c.shape[0]
    n_col = N_MOD * D_MODEL
    tn = D_MODEL
    return pl.pallas_call(
        _mod_kernel,
        grid=(DEPTH, n_col // tn),
        in_specs=[
            pl.BlockSpec((nb, D_MODEL), lambda l, j: (0, 0)),
            pl.BlockSpec((1, D_MODEL, tn), lambda l, j: (l, 0, j)),
            pl.BlockSpec((1, 1, tn), lambda l, j: (l, 0, j)),
        ],
        out_specs=pl.BlockSpec((1, nb, tn), lambda l, j: (l, 0, j)),
        out_shape=jax.ShapeDtypeStruct((DEPTH, nb, n_col), F32),
        compiler_params=pltpu.CompilerParams(dimension_semantics=("arbitrary", "arbitrary")),
        name="modulation",
    )(c, w_ada, b_ada.reshape(DEPTH, 1, n_col))


def _rope_t(x, c, s, q):
    sw = jnp.concatenate([x[q:2 * q], x[0:q], x[3 * q:4 * q], x[2 * q:3 * q]], axis=0)
    return x * c + sw * s


def _token_specs(xs, tm, n_i):
    if len(xs) == 1:
        return [pl.BlockSpec((1, tm, D_MODEL), lambda b, i: (b, i, 0))]
    n_p = xs[0].shape[0]
    return [pl.BlockSpec((1, tm, D_MODEL),
                         lambda b, i: (jnp.minimum(b, n_p - 1), jnp.where(b < n_p, i, n_i - 1), 0)),
            pl.BlockSpec((1, tm, D_MODEL),
                         lambda b, i: (jnp.maximum(b - n_p, 0), jnp.where(b >= n_p, i, 0), 0))]


def _load_tokens(x_refs, n_prompt):
    if len(x_refs) == 1:
        return x_refs[0][0]
    return jnp.where(pl.program_id(0) < n_prompt, x_refs[0][0], x_refs[1][0])


def _attn_in_kernel(*refs, n_x, n_prompt):
    x_refs = refs[:n_x]
    (mod_ref, gat_ref, win_ref, gq_ref, gk_ref, gcq_ref, wuqt_ref, gckv_ref, wuk_ref, wuvt_ref,
     ca_ref, sa_ref, cb_ref, sb_ref, q_ref, k_ref, v_ref) = refs[n_x:]
    x_all = _load_tokens(x_refs, n_prompt)
    tm_all = x_all.shape[0]
    tm = min(SUB_ROWS, tm_all)
    mod = mod_ref[0]
    shift = mod[:, 0:D_MODEL]
    scale = mod[:, D_MODEL:2 * D_MODEL]

    def project(t):
        x = x_all[tm * t:tm * (t + 1)]
        r = _rsqrt_mean(jnp.sum(x * x, axis=-1, keepdims=True), D_MODEL)
        h = (x * r) * gat_ref[0] * (1.0 + scale) + shift
        return jnp.dot(h.astype(BF16), win_ref[0], preferred_element_type=F32)

    def finish(t, proj):
        tok = slice(tm * t, tm * (t + 1))
        ca = ca_ref[:, tok]
        sa = sa_ref[:, tok]
        cb = cb_ref[:, tok]
        sb = sb_ref[:, tok]
        zeros_half = jnp.zeros((HEAD_DIM_A, tm), F32)

        o_qa = 0
        q_at = proj[:, o_qa:o_qa + WIDTH_A].T
        qscale_a = (HEAD_DIM_A ** -0.5) * LOG2E
        gq = gq_ref[0, :, tok]
        for j in range(N_HEADS_A):
            xh = q_at[HEAD_DIM_A * j:HEAD_DIM_A * (j + 1)]
            rh = _rsqrt_mean(jnp.sum(xh * xh, axis=0, keepdims=True), HEAD_DIM_A)
            xr = _rope_t((xh * rh) * gq, ca, sa, HEAD_DIM_A // 4) * qscale_a
            g = j // (N_HEADS_A // N_KV_HEADS_A)
            slot = jnp.concatenate([xr, zeros_half] if g == 0 else [zeros_half, xr], axis=0)
            q_ref[0, HEAD_SLOT * j:HEAD_SLOT * (j + 1), tok] = slot.astype(BF16)

        o_ka = WIDTH_A
        k_at = proj[:, o_ka:o_ka + KV_WIDTH_A].T
        gk = gk_ref[0, :, tok]
        kparts = []
        for g in range(N_KV_HEADS_A):
            xh = k_at[HEAD_DIM_A * g:HEAD_DIM_A * (g + 1)]
            rh = _rsqrt_mean(jnp.sum(xh * xh, axis=0, keepdims=True), HEAD_DIM_A)
            kparts.append(_rope_t((xh * rh) * gk, ca, sa, HEAD_DIM_A // 4))
        k_ref[0, 0, tok, :] = jnp.concatenate(kparts, axis=0).T.astype(BF16)
        o_va = o_ka + KV_WIDTH_A
        ones_rows = jnp.where(
            lax.broadcasted_iota(jnp.int32, (V_SLOT - HEAD_DV, tm), 0) == 0, 1.0, 0.0).astype(BF16)

        def store_values(slot, v_t):
            v_ref[0, V_SLOT * slot:V_SLOT * slot + HEAD_DV, tok] = v_t.astype(BF16)
            v_ref[0, V_SLOT * slot + HEAD_DV:V_SLOT * (slot + 1), tok] = ones_rows

        va_t = proj[:, o_va:o_va + KV_WIDTH_A].T
        for g in range(N_KV_HEADS_A):
            store_values(g, va_t[HEAD_DV * g:HEAD_DV * (g + 1)])

        o_cq = o_va + KV_WIDTH_A
        cq_t = proj[:, o_cq:o_cq + Q_LORA].T
        rq = _rsqrt_mean(jnp.sum(cq_t * cq_t, axis=0, keepdims=True), Q_LORA)
        cqn = (cq_t * rq) * gcq_ref[0, :, tok]
        qb_t = jnp.dot(wuqt_ref[0], cqn.astype(BF16), preferred_element_type=F32)
        qscale_b = ((QK_NOPE_DIM + QK_ROPE_DIM) ** -0.5) * LOG2E
        zeros_pad = jnp.zeros((HEAD_SLOT - QK_NOPE_DIM - QK_ROPE_DIM, tm), F32)
        for j in range(N_HEADS_B):
            base = HEAD_SLOT * j
            nope = qb_t[base:base + QK_NOPE_DIM]
            rope = _rope_t(qb_t[base + ROPE_SLOT_OFF:base + ROPE_SLOT_OFF + QK_ROPE_DIM], cb, sb,
                           QK_ROPE_DIM // 4)
            slot = jnp.concatenate([nope, rope, zeros_pad], axis=0) * qscale_b
            row0 = N_HEADS_A * HEAD_SLOT + base
            q_ref[0, row0:row0 + HEAD_SLOT, tok] = slot.astype(BF16)

        o_ckv = o_cq + Q_LORA
        ckv = proj[:, o_ckv:o_ckv + KV_LORA]
        rkv = _rsqrt_mean(jnp.sum(ckv * ckv, axis=-1, keepdims=True), KV_LORA)
        ckvn = (ckv * rkv) * gckv_ref[0]
        k_nope = jnp.dot(ckvn.astype(BF16), wuk_ref[0], preferred_element_type=F32)
        vb_t = jnp.dot(wuvt_ref[0], ckvn.T.astype(BF16), preferred_element_type=F32)
        for j in range(N_HEADS_B):
            store_values(N_KV_HEADS_A + j, vb_t[HEAD_DV * j:HEAD_DV * (j + 1)])
        o_kr = o_ckv + KV_LORA
        kr_t = proj[:, o_kr:o_kr + LANES].T
        kr_rope = _rope_t(kr_t[ROPE_SLOT_OFF:ROPE_SLOT_OFF + QK_ROPE_DIM], cb, sb, QK_ROPE_DIM // 4)
        kr_tile = jnp.concatenate(
            [jnp.zeros((ROPE_SLOT_OFF, tm), F32), kr_rope, zeros_pad], axis=0).T
        for j in range(N_HEADS_B):
            base = HEAD_SLOT * j
            k_ref[0, 1 + j, tok, :] = (k_nope[:, base:base + HEAD_SLOT] + kr_tile).astype(BF16)

    n_sub = tm_all // tm
    projs = [project(t) for t in range(n_sub)]
    for t in range(n_sub):
        finish(t, projs[t])


def _attn_in(xs, mod, l, w, tabs, tm):
    nb = sum(x.shape[0] for x in xs)
    s = xs[0].shape[1]
    grid = (nb, s // tm)
    lay = lambda *blk: pl.BlockSpec((1,) + blk, lambda b, i: (l,) + (0,) * len(blk))
    tab = lambda rows: pl.BlockSpec((rows, tm), lambda b, i: (0, i))
    out_shapes = (
        jax.ShapeDtypeStruct((nb, N_HEADS * HEAD_SLOT, s), BF16),
        jax.ShapeDtypeStruct((nb, N_KEY_SLABS, s, HEAD_SLOT), BF16),
        jax.ShapeDtypeStruct((nb, V_ROWS, s), BF16),
    )
    return pl.pallas_call(
        functools.partial(_attn_in_kernel, n_x=len(xs), n_prompt=xs[0].shape[0]),
        grid=grid,
        in_specs=_token_specs(xs, tm, s // tm) + [
            pl.BlockSpec((1, 1, N_MOD * D_MODEL), lambda b, i: (b, 0, 0)),
            lay(1, D_MODEL),
            lay(D_MODEL, D_IN_PAD),
            lay(HEAD_DIM_A, tm),
            lay(HEAD_DIM_A, tm),
            lay(Q_LORA, tm),
            lay(N_HEADS_B * HEAD_SLOT, Q_LORA),
            lay(1, KV_LORA),
            lay(KV_LORA, N_HEADS_B * HEAD_SLOT),
            lay(WIDTH_B, KV_LORA),
            tab(HEAD_DIM_A), tab(HEAD_DIM_A), tab(QK_ROPE_DIM), tab(QK_ROPE_DIM),
        ],
        out_specs=(pl.BlockSpec((1, N_HEADS * HEAD_SLOT, tm), lambda b, i: (b, 0, i)),
                   pl.BlockSpec((1, N_KEY_SLABS, tm, HEAD_SLOT), lambda b, i: (b, 0, i, 0)),
                   pl.BlockSpec((1, V_ROWS, tm), lambda b, i: (b, 0, i))),
        out_shape=out_shapes,
        compiler_params=pltpu.CompilerParams(
            dimension_semantics=("arbitrary", "arbitrary"), vmem_limit_bytes=VMEM_LIMIT),
        name="attn_in",
    )(*xs, mod, w["g_attn"], w["w_in"], w["gq_t"], w["gk_t"], w["gcq_t"], w["w_uq_t"], w["g_ckv"],
      w["w_uk"], w["w_uv_t"], tabs["ca"], tabs["sa"], tabs["cb"], tabs["sb"])


def _fold_rows(x, op):
    parts = [x[FOLD_ROWS * i:FOLD_ROWS * (i + 1)] for i in range(x.shape[0] // FOLD_ROWS)]
    while len(parts) > 1:
        parts = [op(parts[2 * i], parts[2 * i + 1]) for i in range(len(parts) // 2)]
    return parts[0]


def _key_slab(h):
    if isinstance(h, int):
        return 0 if h < N_HEADS_A else h - N_HEADS_A + 1
    return jnp.where(h < N_HEADS_A, 0, h - N_HEADS_A + 1)


def _value_row(h):
    per_kv = N_HEADS_A // N_KV_HEADS_A
    if isinstance(h, int):
        return V_SLOT * (h // per_kv if h < N_HEADS_A else h - N_HEADS_A + N_KV_HEADS_A)
    return V_SLOT * jnp.where(h < N_HEADS_A, h // per_kv, h - N_HEADS_A + N_KV_HEADS_A)


def _row_block(start, size):
    if isinstance(start, int):
        return pl.ds(start, size)
    return pl.ds(pl.multiple_of(start, size), size)


def _attention_kernel(q_ref, k_ref, v_ref, ot_ref, s0_ref, s1_ref):
    n_keys = k_ref.shape[2]
    w = s0_ref.shape[1]
    n_parts = ot_ref.shape[2] // w
    n_chunks = n_keys // KEY_CHUNK
    rows = lambda c: slice(KEY_CHUNK * c, KEY_CHUNK * (c + 1))
    cols = lambda t: slice(w * t, w * (t + 1))

    def stage(nxt, s_next_ref, cur, s_cur_ref, carry, fast):
        m_next = None
        if nxt is not None:
            h_next, t_next = nxt
            q_t = q_ref[0, _row_block(h_next * HEAD_SLOT, HEAD_SLOT), cols(t_next)]
            slab = _key_slab(h_next)
        if cur is not None:
            h_cur, t_cur = cur
            if not fast:
                m_b = jnp.broadcast_to(jnp.max(carry, axis=0, keepdims=True), (KEY_CHUNK, w))
            v_rows = _row_block(_value_row(h_cur), V_SLOT)
            o_acc = jnp.zeros((V_SLOT, w), F32)
        for c in range(n_chunks):
            if nxt is not None:
                s = jnp.dot(k_ref[0, slab, rows(c), :], q_t, preferred_element_type=F32)
                s_next_ref[rows(c), :] = s
                if not fast:
                    folded = _fold_rows(s, jnp.maximum)
                    m_next = folded if m_next is None else jnp.maximum(m_next, folded)
            if cur is not None:
                s_cur = s_cur_ref[rows(c), :]
                p = jnp.exp2(s_cur if fast else s_cur - m_b)
                o_acc = o_acc + jnp.dot(v_ref[0, v_rows, rows(c)], p.astype(BF16),
                                        preferred_element_type=F32)
        if cur is not None:
            denom = o_acc[HEAD_DV:HEAD_DV + 1]
            ot_ref[0, _row_block(h_cur * HEAD_DV, HEAD_DV), cols(t_cur)] = (
                o_acc[:HEAD_DV] * (1.0 / denom)).astype(ot_ref.dtype)
            if fast:
                carry = (jnp.minimum(carry[0], denom), jnp.maximum(carry[1], denom))
        return carry if fast else m_next

    bufs = (s0_ref, s1_ref)

    def run(fast):
        def head(h, carry, last=False):
            for t in range(n_parts):
                nxt = (h, t + 1) if t + 1 < n_parts else (None if last else (h + 1, 0))
                carry = stage(nxt, bufs[(t + 1) % 2] if nxt is not None else None, (h, t), bufs[t % 2],
                              carry, fast)
            return carry

        init = (jnp.ones((1, w), F32), jnp.ones((1, w), F32)) if fast else None
        carry = lax.fori_loop(0, N_HEADS - 1, head, stage((0, 0), s0_ref, None, None, init, fast))
        return head(N_HEADS - 1, carry, last=True)

    d_min, d_max = run(fast=True)
    in_range = jnp.logical_and(d_min >= 2.0 ** -SAFE_LOG2, d_max <= 2.0 ** SAFE_LOG2)
    redo = jnp.min(jnp.where(in_range, 1, 0)) == 0

    @pl.when(redo)
    def _():
        run(fast=False)


def _attention(q_all, k_all, v_all, tq):
    nb, _, s, _ = k_all.shape
    w = min(QUERY_ITEM, tq // 2)
    assert s % KEY_CHUNK == 0 and w % LANES == 0 and tq % (2 * w) == 0
    score_buf = pltpu.VMEM((s, w), F32)
    return pl.pallas_call(
        _attention_kernel,
        grid=(nb, s // tq),
        in_specs=[pl.BlockSpec((1, N_HEADS * HEAD_SLOT, tq), lambda b, i: (b, 0, i)),
                  pl.BlockSpec((1, N_KEY_SLABS, s, HEAD_SLOT), lambda b, i: (b, 0, 0, 0)),
                  pl.BlockSpec((1, V_ROWS, s), lambda b, i: (b, 0, 0))],
        out_specs=pl.BlockSpec((1, D_MIX, tq), lambda b, i: (b, 0, i)),
        out_shape=jax.ShapeDtypeStruct((nb, D_MIX, s), BF16),
        scratch_shapes=[score_buf, score_buf],
        compiler_params=pltpu.CompilerParams(
            dimension_semantics=("arbitrary", "arbitrary"), vmem_limit_bytes=VMEM_LIMIT),
        name="attention",
    )(q_all, k_all, v_all)


def _attn_out_kernel(ot_ref, *refs, n_x, n_prompt):
    x_refs = refs[:n_x]
    mod_ref, go_ref, wo_ref, gffn_ref, x1_ref, h2_ref = refs[n_x:]
    o_t = ot_ref[0].astype(F32)
    oa = o_t[:WIDTH_A]
    ob = o_t[WIDTH_A:]
    ra = _rsqrt_mean(jnp.sum(oa * oa, axis=0, keepdims=True), WIDTH_A)
    rb = _rsqrt_mean(jnp.sum(ob * ob, axis=0, keepdims=True), WIDTH_B)
    on = jnp.concatenate([oa * ra, ob * rb], axis=0).T * go_ref[0]
    res = jnp.dot(on.astype(BF16), wo_ref[0], preferred_element_type=F32)
    mod = mod_ref[0]
    gate_a = mod[:, 2 * D_MODEL:3 * D_MODEL]
    shift_f = mod[:, 3 * D_MODEL:4 * D_MODEL]
    scale_f = mod[:, 4 * D_MODEL:5 * D_MODEL]
    x1 = _load_tokens(x_refs, n_prompt) + gate_a * res
    x1_ref[0] = x1
    r = _rsqrt_mean(jnp.sum(x1 * x1, axis=-1, keepdims=True), D_MODEL)
    h2_ref[0] = ((x1 * r) * gffn_ref[0] * (1.0 + scale_f) + shift_f).astype(BF16)


def _attn_out(o_t, xs, mod, l, w, tm):
    nb, _, s = o_t.shape
    lay = lambda *blk: pl.BlockSpec((1,) + blk, lambda b, i: (l,) + (0,) * len(blk))
    tok = pl.BlockSpec((1, tm, D_MODEL), lambda b, i: (b, i, 0))
    return pl.pallas_call(
        functools.partial(_attn_out_kernel, n_x=len(xs), n_prompt=xs[0].shape[0]),
        grid=(nb, s // tm),
        in_specs=[pl.BlockSpec((1, D_MIX, tm), lambda b, i: (b, 0, i))] + _token_specs(xs, tm, s // tm) + [
            pl.BlockSpec((1, 1, N_MOD * D_MODEL), lambda b, i: (b, 0, 0)),
            lay(1, D_MIX),
            lay(D_MIX, D_MODEL),
            lay(1, D_MODEL),
        ],
        out_specs=(tok, tok),
        out_shape=(jax.ShapeDtypeStruct((nb, s, D_MODEL), F32), jax.ShapeDtypeStruct((nb, s, D_MODEL), BF16)),
        compiler_params=pltpu.CompilerParams(
            dimension_semantics=("arbitrary", "arbitrary"), vmem_limit_bytes=VMEM_LIMIT),
        name="attn_out",
    )(o_t, *xs, mod, w["g_out"], w["w_o"], w["g_ffn"])


def _ffn_kernel(hm_ref, hp_ref, hn_ref, x1_ref, mod_ref, wup_ref, cw_ref, cb_ref, wdn_ref, gfin_ref,
                out_ref, hext_ref, ua_ref, ub_ref, *, final):
    acc_ref = out_ref.at[0]
    i = pl.program_id(1)
    n_i = pl.num_programs(1)
    tm = hm_ref.shape[1]

    keep_prev = jnp.where(i > 0, 1.0, 0.0).astype(F32)
    keep_next = jnp.where(i < n_i - 1, 1.0, 0.0).astype(F32)
    hext_ref[0:HALO, :] = (hp_ref[0].astype(F32) * keep_prev).astype(BF16)
    hext_ref[HALO:HALO + tm, :] = hm_ref[0]
    hext_ref[HALO + tm:HALO + tm + HALO, :] = (hn_ref[0].astype(F32) * keep_next).astype(BF16)
    acc_ref[...] = jnp.zeros((tm, D_MODEL), F32)

    def up(c, u_ref):
        u = jnp.dot(hext_ref[...], wup_ref[0, c], preferred_element_type=F32)
        for j in range(FF_SLABS):
            u_ref[j] = u[:, LANES * j:LANES * (j + 1)]

    def mix(c, u_ref):
        cw = cw_ref[0, c]
        cb = cb_ref[0, c]
        ys = []
        for j in range(FF_SLABS):
            lanes = slice(LANES * j, LANES * (j + 1))
            ys.append(u_ref[j, HALO - 1:HALO - 1 + tm, :] * cw[0:1, lanes]
                      + u_ref[j, HALO:HALO + tm, :] * cw[1:2, lanes]
                      + u_ref[j, HALO + 1:HALO + 1 + tm, :] * cw[2:3, lanes] + cb[:, lanes])
        half = FF_SLABS // 2
        acts = []
        for j in range(half):
            gate = ys[half + j]
            acts.append((gate * (1.0 / (1.0 + jnp.exp(-gate)))) * ys[j])
        act = jnp.concatenate(acts, axis=1).astype(BF16)
        acc_ref[...] += jnp.dot(act, wdn_ref[0, c], preferred_element_type=F32)

    def pair(k, carry):
        c = 2 * k
        up(c + 1, ub_ref)
        mix(c, ua_ref)
        up(c + 2, ua_ref)
        mix(c + 1, ub_ref)
        return carry

    assert N_FF_CHUNKS % 2 == 1
    up(0, ua_ref)
    lax.fori_loop(0, N_FF_CHUNKS // 2, pair, 0)
    mix(N_FF_CHUNKS - 1, ua_ref)

    gate_f = mod_ref[0][:, 5 * D_MODEL:6 * D_MODEL]
    x2 = x1_ref[0] + gate_f * acc_ref[...]
    if final:
        r = _rsqrt_mean(jnp.sum(x2 * x2, axis=-1, keepdims=True), D_MODEL)
        x2 = (x2 * r) * gfin_ref[...]
    out_ref[0] = x2


def _ffn(h2, x1, mod, l, w, g_final, tm, final, b0=0, nb=None):
    nb = x1.shape[0] if nb is None else nb
    s = x1.shape[1]
    n_i = s // tm
    hb = tm // HALO
    tok = pl.BlockSpec((1, tm, D_MODEL), lambda b, i: (b + b0, i, 0))
    resident = lambda *blk: pl.BlockSpec((1,) + blk, lambda b, i: (l,) + (0,) * len(blk),
                                         pipeline_mode=pl.Buffered(1))
    u_buf = pltpu.VMEM((FF_SLABS, tm + 2 * HALO, LANES), F32)
    return pl.pallas_call(
        functools.partial(_ffn_kernel, final=final),
        grid=(nb, n_i),
        in_specs=[
            tok,
            pl.BlockSpec((1, HALO, D_MODEL), lambda b, i: (b + b0, jnp.maximum(i * hb - 1, 0), 0)),
            pl.BlockSpec((1, HALO, D_MODEL),
                         lambda b, i: (b + b0, jnp.minimum((i + 1) * hb, s // HALO - 1), 0)),
            tok,
            pl.BlockSpec((1, 1, N_MOD * D_MODEL), lambda b, i: (b + b0, 0, 0)),
            resident(N_FF_CHUNKS, D_MODEL, 2 * FF_CHUNK),
            resident(N_FF_CHUNKS, 3, 2 * FF_CHUNK),
            resident(N_FF_CHUNKS, 1, 2 * FF_CHUNK),
            resident(N_FF_CHUNKS, FF_CHUNK, D_MODEL),
            pl.BlockSpec((1, D_MODEL), lambda b, i: (0, 0)),
        ],
        out_specs=pl.BlockSpec((1, tm, D_MODEL), lambda b, i: (b, i, 0)),
        out_shape=jax.ShapeDtypeStruct((nb, s, D_MODEL), F32),
        scratch_shapes=[pltpu.VMEM((tm + 2 * HALO, D_MODEL), BF16), u_buf, u_buf],
        compiler_params=pltpu.CompilerParams(
            dimension_semantics=("arbitrary", "arbitrary"), vmem_limit_bytes=VMEM_LIMIT),
        name="ffn",
    )(h2, h2, h2, x1, mod, w["w_up"], w["conv_w"], w["conv_b"], w["w_down"], g_final)


def _rope_tables_t(n_tokens, rot_dim):
    rows = n_tokens // GRID_W
    row = jnp.repeat(jnp.arange(rows, dtype=F32), GRID_W)
    col = jnp.tile(jnp.arange(GRID_W, dtype=F32), rows)
    quarter = rot_dim // 4
    inv_freq = ROPE_THETA ** (-jnp.arange(quarter, dtype=F32) / quarter)
    ang = jnp.stack([row, col], axis=-1)[:, :, None] * inv_freq
    cos, sin = jnp.cos(ang), jnp.sin(ang)
    c = jnp.concatenate([cos[:, 0], cos[:, 0], cos[:, 1], cos[:, 1]], axis=-1).T
    s = jnp.concatenate([-sin[:, 0], sin[:, 0], -sin[:, 1], sin[:, 1]], axis=-1).T
    return c, s


def _prepare_weights(g_attn, w_in, g_q_a, g_k_a, g_cq, w_uq, g_ckv, w_ukv, g_out_a, g_out_b, w_o,
                     g_ffn, w_up, conv_w, conv_b, w_down, tm):
    o_kr = D_IN - QK_ROPE_DIM
    w_in_p = jnp.concatenate([
        w_in[..., :o_kr], jnp.zeros((DEPTH, D_MODEL, ROPE_SLOT_OFF), F32), w_in[..., o_kr:],
        jnp.zeros((DEPTH, D_MODEL, LANES - ROPE_SLOT_OFF - QK_ROPE_DIM), F32)], axis=-1).astype(BF16)
    hq = QK_NOPE_DIM + QK_ROPE_DIM
    w_uq_p = jnp.pad(w_uq.reshape(DEPTH, Q_LORA, N_HEADS_B, hq), ((0, 0), (0, 0), (0, 0), (0, HEAD_SLOT - hq)))
    w_uq_t = w_uq_p.reshape(DEPTH, Q_LORA, N_HEADS_B * HEAD_SLOT).transpose(0, 2, 1).astype(BF16)
    w_ukv_h = w_ukv.reshape(DEPTH, KV_LORA, N_HEADS_B, QK_NOPE_DIM + V_DIM_B)
    w_uk = jnp.pad(w_ukv_h[..., :QK_NOPE_DIM], ((0, 0), (0, 0), (0, 0), (0, HEAD_SLOT - QK_NOPE_DIM)))
    w_uk = w_uk.reshape(DEPTH, KV_LORA, N_HEADS_B * HEAD_SLOT).astype(BF16)
    w_uv_t = w_ukv_h[..., QK_NOPE_DIM:].reshape(DEPTH, KV_LORA, WIDTH_B).transpose(0, 2, 1).astype(BF16)
    def chunked(a):
        cols = lambda c, off: a[..., off + FF_CHUNK * c:off + FF_CHUNK * (c + 1)]
        return jnp.stack([jnp.concatenate([cols(c, 0), cols(c, D_FF)], axis=-1) for c in range(N_FF_CHUNKS)],
                         axis=1)

    w_up_c = chunked(w_up.astype(BF16))
    conv_w_c = chunked(conv_w)
    conv_b_c = chunked(conv_b[:, None, :])
    w_down_c = w_down.reshape(DEPTH, N_FF_CHUNKS, FF_CHUNK, D_MODEL).astype(BF16)
    col = lambda g: jnp.broadcast_to(g[:, :, None], g.shape + (tm,))
    return {
        "g_attn": g_attn.reshape(DEPTH, 1, D_MODEL), "w_in": w_in_p,
        "gq_t": col(g_q_a), "gk_t": col(g_k_a), "gcq_t": col(g_cq), "w_uq_t": w_uq_t,
        "g_ckv": g_ckv.reshape(DEPTH, 1, KV_LORA), "w_uk": w_uk, "w_uv_t": w_uv_t,
        "g_out": jnp.concatenate([g_out_a, g_out_b], axis=-1).reshape(DEPTH, 1, D_MIX),
        "w_o": w_o.astype(BF16), "g_ffn": g_ffn.reshape(DEPTH, 1, D_MODEL),
        "w_up": w_up_c, "conv_w": conv_w_c, "conv_b": conv_b_c, "w_down": w_down_c,
    }


def _tiles(s):
    return min(1024, s), min(2048, s), min(1024, s)


def kernel(x_prompt, x_sample, c_prompt, c_sample, w_ada, b_ada, g_attn, w_in, g_q_a, g_k_a, g_cq, w_uq,
           g_ckv, w_ukv, g_out_a, g_out_b, w_o, g_ffn, w_up, conv_w, conv_b, w_down, g_final):
    n_prompt = x_prompt.shape[0]
    s = x_prompt.shape[1]
    assert x_sample.shape[1] == s and s % GRID_W == 0
    tm, tq, tf = _tiles(s)
    assert all(s % t == 0 for t in (tm, tq, tf)) and tf % HALO == 0
    xs = (x_prompt, x_sample)
    c = jnp.concatenate([c_prompt, c_sample], axis=0)
    nb = c.shape[0]

    w = _prepare_weights(g_attn, w_in, g_q_a, g_k_a, g_cq, w_uq, g_ckv, w_ukv, g_out_a, g_out_b, w_o,
                         g_ffn, w_up, conv_w, conv_b, w_down, tm)
    ca, sa = _rope_tables_t(s, HEAD_DIM_A)
    cb, sb = _rope_tables_t(s, QK_ROPE_DIM)
    tabs = {"ca": ca, "sa": sa, "cb": cb, "sb": sb}
    mod_all = _modulation(c, w_ada, b_ada)
    g_fin = g_final.reshape(1, D_MODEL)

    for l in range(DEPTH):
        mod = mod_all[l].reshape(nb, 1, N_MOD * D_MODEL)
        q_all, k_all, v_all = _attn_in(xs, mod, l, w, tabs, tm)
        o_t = _attention(q_all, k_all, v_all, tq)
        x1, h2 = _attn_out(o_t, xs, mod, l, w, tm)
        if l < DEPTH - 1:
            xs = (_ffn(h2, x1, mod, l, w, g_fin, tf, final=False),)
    return (_ffn(h2, x1, mod, DEPTH - 1, w, g_fin, tf, final=True, b0=0, nb=n_prompt),
            _ffn(h2, x1, mod, DEPTH - 1, w, g_fin, tf, final=True, b0=n_prompt, nb=nb - n_prompt))
```

```python
import functools
import math

import jax
import jax.numpy as jnp
from jax import lax
from jax.experimental import pallas as pl
from jax.experimental.pallas import tpu as pltpu

F32 = jnp.float32
BF16 = jnp.bfloat16

D_MODEL = 1024
DEPTH = 4
GRID_W = 64
ROPE_THETA = 10000.0
EPS = 1e-6
N_HEADS_A = 8
N_KV_HEADS_A = 2
HEAD_DIM_A = 64
N_HEADS_B = 8
QK_NOPE_DIM = 64
QK_ROPE_DIM = 32
V_DIM_B = 64
Q_LORA = 256
KV_LORA = 128
WIDTH_A = N_HEADS_A * HEAD_DIM_A
WIDTH_B = N_HEADS_B * V_DIM_B
D_MIX = WIDTH_A + WIDTH_B
KV_WIDTH_A = N_KV_HEADS_A * HEAD_DIM_A
D_IN = WIDTH_A + 2 * KV_WIDTH_A + Q_LORA + KV_LORA + QK_ROPE_DIM
D_FF = 2816
N_MOD = 6

LANES = 128
HEAD_SLOT = 128
D_IN_PAD = 1280
ROPE_SLOT_OFF = 64
FF_CHUNK = 256
N_FF_CHUNKS = D_FF // FF_CHUNK
FF_SLABS = 2 * FF_CHUNK // LANES
HALO = 16
N_HEADS = N_HEADS_A + N_HEADS_B
N_KEY_SLABS = 1 + N_HEADS_B
HEAD_DV = HEAD_DIM_A
assert HEAD_DV == V_DIM_B
V_ROWS = (N_KV_HEADS_A + N_HEADS_B) * HEAD_DV
QUERY_ITEM = 512
KEY_CHUNK = 256
SAFE_LOG2 = 40
FOLD_ROWS = 8
SUB_ROWS = 256
LOG2E = math.log2(math.e)
VMEM_LIMIT = 56 * 1024 * 1024


def _rsqrt_mean(ss, n):
    return lax.rsqrt(ss * (1.0 / n) + EPS)


def _mod_kernel(c_ref, w_ref, b_ref, o_ref):
    c = c_ref[...]
    ca = c * (1.0 / (1.0 + jnp.exp(-c)))
    o_ref[0] = jnp.dot(ca.astype(BF16), w_ref[0].astype(BF16), preferred_element_type=F32) + b_ref[0]


def _modulation(c, w_ada, b_ada):
    nb = c.shape[0]
    n_col = N_MOD * D_MODEL
    tn = D_MODEL
    return pl.pallas_call(
        _mod_kernel,
        grid=(DEPTH, n_col // tn),
        in_specs=[
            pl.BlockSpec((nb, D_MODEL), lambda l, j: (0, 0)),
            pl.BlockSpec((1, D_MODEL, tn), lambda l, j: (l, 0, j)),
            pl.BlockSpec((1, 1, tn), lambda l, j: (l, 0, j)),
        ],
        out_specs=pl.BlockSpec((1, nb, tn), lambda l, j: (l, 0, j)),
        out_shape=jax.ShapeDtypeStruct((DEPTH, nb, n_col), F32),
        compiler_params=pltpu.CompilerParams(dimension_semantics=("arbitrary", "arbitrary")),
        name="modulation",
    )(c, w_ada, b_ada.reshape(DEPTH, 1, n_col))


def _rope_t(x, c, s, q):
    sw = jnp.concatenate([x[q:2 * q], x[0:q], x[3 * q:4 * q], x[2 * q:3 * q]], axis=0)
    return x * c + sw * s


def _token_specs(xs, tm, n_i):
    if len(xs) == 1:
        return [pl.BlockSpec((1, tm, D_MODEL), lambda b, i: (b, i, 0))]
    n_p = xs[0].shape[0]
    return [pl.BlockSpec((1, tm, D_MODEL),
                         lambda b, i: (jnp.minimum(b, n_p - 1), jnp.where(b < n_p, i, n_i - 1), 0)),
            pl.BlockSpec((1, tm, D_MODEL),
                         lambda b, i: (jnp.maximum(b - n_p, 0), jnp.where(b >= n_p, i, 0), 0))]


def _load_tokens(x_refs, n_prompt):
    if len(x_refs) == 1:
        return x_refs[0][0]
    return jnp.where(pl.program_id(0) < n_prompt, x_refs[0][0], x_refs[1][0])


def _attn_in_kernel(*refs, n_x, n_prompt):
    x_refs = refs[:n_x]
    (mod_ref, gat_ref, win_ref, gq_ref, gk_ref, gcq_ref, wuqt_ref, gckv_ref, wuk_ref, wuvt_ref,
     ca_ref, sa_ref, cb_ref, sb_ref, q_ref, k_ref, v_ref) = refs[n_x:]
    x_all = _load_tokens(x_refs, n_prompt)
    tm_all = x_all.shape[0]
    tm = min(SUB_ROWS, tm_all)
    mod = mod_ref[0]
    shift = mod[:, 0:D_MODEL]
    scale = mod[:, D_MODEL:2 * D_MODEL]

    def project(t):
        x = x_all[tm * t:tm * (t + 1)]
        r = _rsqrt_mean(jnp.sum(x * x, axis=-1, keepdims=True), D_MODEL)
        h = (x * r) * gat_ref[0] * (1.0 + scale) + shift
        return jnp.dot(h.astype(BF16), win_ref[0], preferred_element_type=F32)

    def finish(t, proj):
        tok = slice(tm * t, tm * (t + 1))
        ca = ca_ref[:, tok]
        sa = sa_ref[:, tok]
        cb = cb_ref[:, tok]
        sb = sb_ref[:, tok]
        zeros_half = jnp.zeros((HEAD_DIM_A, tm), F32)

        o_qa = 0
        q_at = proj[:, o_qa:o_qa + WIDTH_A].T
        qscale_a = (HEAD_DIM_A ** -0.5) * LOG2E
        gq = gq_ref[0, :, tok]
        for j in range(N_HEADS_A):
            xh = q_at[HEAD_DIM_A * j:HEAD_DIM_A * (j + 1)]
            rh = _rsqrt_mean(jnp.sum(xh * xh, axis=0, keepdims=True), HEAD_DIM_A)
            xr = _rope_t((xh * rh) * gq, ca, sa, HEAD_DIM_A // 4) * qscale_a
            g = j // (N_HEADS_A // N_KV_HEADS_A)
            slot = jnp.concatenate([xr, zeros_half] if g == 0 else [zeros_half, xr], axis=0)
            q_ref[0, HEAD_SLOT * j:HEAD_SLOT * (j + 1), tok] = slot.astype(BF16)

        o_ka = WIDTH_A
        k_at = proj[:, o_ka:o_ka + KV_WIDTH_A].T
        gk = gk_ref[0, :, tok]
        kparts = []
        for g in range(N_KV_HEADS_A):
            xh = k_at[HEAD_DIM_A * g:HEAD_DIM_A * (g + 1)]
            rh = _rsqrt_mean(jnp.sum(xh * xh, axis=0, keepdims=True), HEAD_DIM_A)
            kparts.append(_rope_t((xh * rh) * gk, ca, sa, HEAD_DIM_A // 4))
        k_ref[0, 0, tok, :] = jnp.concatenate(kparts, axis=0).T.astype(BF16)
        o_va = o_ka + KV_WIDTH_A
        v_ref[0, 0:KV_WIDTH_A, tok] = proj[:, o_va:o_va + KV_WIDTH_A].T.astype(BF16)

        o_cq = o_va + KV_WIDTH_A
        cq_t = proj[:, o_cq:o_cq + Q_LORA].T
        rq = _rsqrt_mean(jnp.sum(cq_t * cq_t, axis=0, keepdims=True), Q_LORA)
        cqn = (cq_t * rq) * gcq_ref[0, :, tok]
        qb_t = jnp.dot(wuqt_ref[0], cqn.astype(BF16), preferred_element_type=F32)
        qscale_b = ((QK_NOPE_DIM + QK_ROPE_DIM) ** -0.5) * LOG2E
        zeros_pad = jnp.zeros((HEAD_SLOT - QK_NOPE_DIM - QK_ROPE_DIM, tm), F32)
        for j in range(N_HEADS_B):
            base = HEAD_SLOT * j
            nope = qb_t[base:base + QK_NOPE_DIM]
            rope = _rope_t(qb_t[base + ROPE_SLOT_OFF:base + ROPE_SLOT_OFF + QK_ROPE_DIM], cb, sb,
                           QK_ROPE_DIM // 4)
            slot = jnp.concatenate([nope, rope, zeros_pad], axis=0) * qscale_b
            row0 = N_HEADS_A * HEAD_SLOT + base
            q_ref[0, row0:row0 + HEAD_SLOT, tok] = slot.astype(BF16)

        o_ckv = o_cq + Q_LORA
        ckv = proj[:, o_ckv:o_ckv + KV_LORA]
        rkv = _rsqrt_mean(jnp.sum(ckv * ckv, axis=-1, keepdims=True), KV_LORA)
        ckvn = (ckv * rkv) * gckv_ref[0]
        k_nope = jnp.dot(ckvn.astype(BF16), wuk_ref[0], preferred_element_type=F32)
        v_ref[0, KV_WIDTH_A:V_ROWS, tok] = jnp.dot(
            wuvt_ref[0], ckvn.T.astype(BF16), preferred_element_type=F32).astype(BF16)
        o_kr = o_ckv + KV_LORA
        kr_t = proj[:, o_kr:o_kr + LANES].T
        kr_rope = _rope_t(kr_t[ROPE_SLOT_OFF:ROPE_SLOT_OFF + QK_ROPE_DIM], cb, sb, QK_ROPE_DIM // 4)
        kr_tile = jnp.concatenate(
            [jnp.zeros((ROPE_SLOT_OFF, tm), F32), kr_rope, zeros_pad], axis=0).T
        for j in range(N_HEADS_B):
            base = HEAD_SLOT * j
            k_ref[0, 1 + j, tok, :] = (k_nope[:, base:base + HEAD_SLOT] + kr_tile).astype(BF16)

    n_sub = tm_all // tm
    projs = [project(t) for t in range(n_sub)]
    for t in range(n_sub):
        finish(t, projs[t])


def _attn_in(xs, mod, l, w, tabs, tm):
    nb = sum(x.shape[0] for x in xs)
    s = xs[0].shape[1]
    grid = (nb, s // tm)
    lay = lambda *blk: pl.BlockSpec((1,) + blk, lambda b, i: (l,) + (0,) * len(blk))
    tab = lambda rows: pl.BlockSpec((rows, tm), lambda b, i: (0, i))
    out_shapes = (
        jax.ShapeDtypeStruct((nb, N_HEADS * HEAD_SLOT, s), BF16),
        jax.ShapeDtypeStruct((nb, N_KEY_SLABS, s, HEAD_SLOT), BF16),
        jax.ShapeDtypeStruct((nb, V_ROWS, s), BF16),
    )
    return pl.pallas_call(
        functools.partial(_attn_in_kernel, n_x=len(xs), n_prompt=xs[0].shape[0]),
        grid=grid,
        in_specs=_token_specs(xs, tm, s // tm) + [
            pl.BlockSpec((1, 1, N_MOD * D_MODEL), lambda b, i: (b, 0, 0)),
            lay(1, D_MODEL),
            lay(D_MODEL, D_IN_PAD),
            lay(HEAD_DIM_A, tm),
            lay(HEAD_DIM_A, tm),
            lay(Q_LORA, tm),
            lay(N_HEADS_B * HEAD_SLOT, Q_LORA),
            lay(1, KV_LORA),
            lay(KV_LORA, N_HEADS_B * HEAD_SLOT),
            lay(WIDTH_B, KV_LORA),
            tab(HEAD_DIM_A), tab(HEAD_DIM_A), tab(QK_ROPE_DIM), tab(QK_ROPE_DIM),
        ],
        out_specs=(pl.BlockSpec((1, N_HEADS * HEAD_SLOT, tm), lambda b, i: (b, 0, i)),
                   pl.BlockSpec((1, N_KEY_SLABS, tm, HEAD_SLOT), lambda b, i: (b, 0, i, 0)),
                   pl.BlockSpec((1, V_ROWS, tm), lambda b, i: (b, 0, i))),
        out_shape=out_shapes,
        compiler_params=pltpu.CompilerParams(
            dimension_semantics=("arbitrary", "arbitrary"), vmem_limit_bytes=VMEM_LIMIT),
        name="attn_in",
    )(*xs, mod, w["g_attn"], w["w_in"], w["gq_t"], w["gk_t"], w["gcq_t"], w["w_uq_t"], w["g_ckv"],
      w["w_uk"], w["w_uv_t"], tabs["ca"], tabs["sa"], tabs["cb"], tabs["sb"])


def _fold_rows(x, op):
    parts = [x[FOLD_ROWS * i:FOLD_ROWS * (i + 1)] for i in range(x.shape[0] // FOLD_ROWS)]
    while len(parts) > 1:
        parts = [op(parts[2 * i], parts[2 * i + 1]) for i in range(len(parts) // 2)]
    return parts[0]


def _key_slab(h):
    if isinstance(h, int):
        return 0 if h < N_HEADS_A else h - N_HEADS_A + 1
    return jnp.where(h < N_HEADS_A, 0, h - N_HEADS_A + 1)


def _value_row(h):
    per_kv = N_HEADS_A // N_KV_HEADS_A
    if isinstance(h, int):
        return HEAD_DV * (h // per_kv if h < N_HEADS_A else h - N_HEADS_A + N_KV_HEADS_A)
    return HEAD_DV * jnp.where(h < N_HEADS_A, h // per_kv, h - N_HEADS_A + N_KV_HEADS_A)


def _row_block(start, size):
    if isinstance(start, int):
        return pl.ds(start, size)
    return pl.ds(pl.multiple_of(start, size), size)


def _attention_kernel(q_ref, k_ref, v_ref, ot_ref, s0_ref, s1_ref):
    n_keys = k_ref.shape[2]
    w = s0_ref.shape[1]
    n_parts = ot_ref.shape[2] // w
    n_chunks = n_keys // KEY_CHUNK
    rows = lambda c: slice(KEY_CHUNK * c, KEY_CHUNK * (c + 1))
    cols = lambda t: slice(w * t, w * (t + 1))

    def stage(nxt, s_next_ref, cur, s_cur_ref, carry, fast):
        m_next = None
        if nxt is not None:
            h_next, t_next = nxt
            q_t = q_ref[0, _row_block(h_next * HEAD_SLOT, HEAD_SLOT), cols(t_next)]
            slab = _key_slab(h_next)
        if cur is not None:
            h_cur, t_cur = cur
            if not fast:
                m_b = jnp.broadcast_to(jnp.max(carry, axis=0, keepdims=True), (KEY_CHUNK, w))
            v_rows = _row_block(_value_row(h_cur), HEAD_DV)
            l_acc = jnp.zeros((FOLD_ROWS, w), F32)
            o_acc = jnp.zeros((HEAD_DV, w), F32)
        for c in range(n_chunks):
            if nxt is not None:
                s = jnp.dot(k_ref[0, slab, rows(c), :], q_t, preferred_element_type=F32)
                s_next_ref[rows(c), :] = s
                if not fast:
                    folded = _fold_rows(s, jnp.maximum)
                    m_next = folded if m_next is None else jnp.maximum(m_next, folded)
            if cur is not None:
                s_cur = s_cur_ref[rows(c), :]
                p = jnp.exp2(s_cur if fast else s_cur - m_b)
                l_acc = l_acc + _fold_rows(p, jnp.add)
                o_acc = o_acc + jnp.dot(v_ref[0, v_rows, rows(c)], p.astype(BF16),
                                        preferred_element_type=F32)
        if cur is not None:
            denom = jnp.sum(l_acc, axis=0, keepdims=True)
            ot_ref[0, _row_block(h_cur * HEAD_DV, HEAD_DV), cols(t_cur)] = (
                o_acc * (1.0 / denom)).astype(ot_ref.dtype)
            if fast:
                carry = (jnp.minimum(carry[0], denom), jnp.maximum(carry[1], denom))
        return carry if fast else m_next

    bufs = (s0_ref, s1_ref)

    def run(fast):
        def head(h, carry, last=False):
            for t in range(n_parts):
                nxt = (h, t + 1) if t + 1 < n_parts else (None if last else (h + 1, 0))
                carry = stage(nxt, bufs[(t + 1) % 2] if nxt is not None else None, (h, t), bufs[t % 2],
                              carry, fast)
            return carry

        init = (jnp.ones((1, w), F32), jnp.ones((1, w), F32)) if fast else None
        carry = lax.fori_loop(0, N_HEADS - 1, head, stage((0, 0), s0_ref, None, None, init, fast))
        return head(N_HEADS - 1, carry, last=True)

    d_min, d_max = run(fast=True)
    in_range = jnp.logical_and(d_min >= 2.0 ** -SAFE_LOG2, d_max <= 2.0 ** SAFE_LOG2)
    redo = jnp.min(jnp.where(in_range, 1, 0)) == 0

    @pl.when(redo)
    def _():
        run(fast=False)


def _attention(q_all, k_all, v_all, tq):
    nb, _, s, _ = k_all.shape
    w = min(QUERY_ITEM, tq // 2)
    assert s % KEY_CHUNK == 0 and w % LANES == 0 and tq % (2 * w) == 0
    score_buf = pltpu.VMEM((s, w), F32)
    return pl.pallas_call(
        _attention_kernel,
        grid=(nb, s // tq),
        in_specs=[pl.BlockSpec((1, N_HEADS * HEAD_SLOT, tq), lambda b, i: (b, 0, i)),
                  pl.BlockSpec((1, N_KEY_SLABS, s, HEAD_SLOT), lambda b, i: (b, 0, 0, 0)),
                  pl.BlockSpec((1, V_ROWS, s), lambda b, i: (b, 0, 0))],
        out_specs=pl.BlockSpec((1, D_MIX, tq), lambda b, i: (b, 0, i)),
        out_shape=jax.ShapeDtypeStruct((nb, D_MIX, s), BF16),
        scratch_shapes=[score_buf, score_buf],
        compiler_params=pltpu.CompilerParams(
            dimension_semantics=("arbitrary", "arbitrary"), vmem_limit_bytes=VMEM_LIMIT),
        name="attention",
    )(q_all, k_all, v_all)


def _attn_out_kernel(ot_ref, *refs, n_x, n_prompt):
    x_refs = refs[:n_x]
    mod_ref, go_ref, wo_ref, gffn_ref, x1_ref, h2_ref = refs[n_x:]
    o_t = ot_ref[0].astype(F32)
    oa = o_t[:WIDTH_A]
    ob = o_t[WIDTH_A:]
    ra = _rsqrt_mean(jnp.sum(oa * oa, axis=0, keepdims=True), WIDTH_A)
    rb = _rsqrt_mean(jnp.sum(ob * ob, axis=0, keepdims=True), WIDTH_B)
    on = jnp.concatenate([oa * ra, ob * rb], axis=0).T * go_ref[0]
    res = jnp.dot(on.astype(BF16), wo_ref[0], preferred_element_type=F32)
    mod = mod_ref[0]
    gate_a = mod[:, 2 * D_MODEL:3 * D_MODEL]
    shift_f = mod[:, 3 * D_MODEL:4 * D_MODEL]
    scale_f = mod[:, 4 * D_MODEL:5 * D_MODEL]
    x1 = _load_tokens(x_refs, n_prompt) + gate_a * res
    x1_ref[0] = x1
    r = _rsqrt_mean(jnp.sum(x1 * x1, axis=-1, keepdims=True), D_MODEL)
    h2_ref[0] = ((x1 * r) * gffn_ref[0] * (1.0 + scale_f) + shift_f).astype(BF16)


def _attn_out(o_t, xs, mod, l, w, tm):
    nb, _, s = o_t.shape
    lay = lambda *blk: pl.BlockSpec((1,) + blk, lambda b, i: (l,) + (0,) * len(blk))
    tok = pl.BlockSpec((1, tm, D_MODEL), lambda b, i: (b, i, 0))
    return pl.pallas_call(
        functools.partial(_attn_out_kernel, n_x=len(xs), n_prompt=xs[0].shape[0]),
        grid=(nb, s // tm),
        in_specs=[pl.BlockSpec((1, D_MIX, tm), lambda b, i: (b, 0, i))] + _token_specs(xs, tm, s // tm) + [
            pl.BlockSpec((1, 1, N_MOD * D_MODEL), lambda b, i: (b, 0, 0)),
            lay(1, D_MIX),
            lay(D_MIX, D_MODEL),
            lay(1, D_MODEL),
        ],
        out_specs=(tok, tok),
        out_shape=(jax.ShapeDtypeStruct((nb, s, D_MODEL), F32), jax.ShapeDtypeStruct((nb, s, D_MODEL), BF16)),
        compiler_params=pltpu.CompilerParams(
            dimension_semantics=("arbitrary", "arbitrary"), vmem_limit_bytes=VMEM_LIMIT),
        name="attn_out",
    )(o_t, *xs, mod, w["g_out"], w["w_o"], w["g_ffn"])


def _ffn_kernel(hm_ref, hp_ref, hn_ref, x1_ref, mod_ref, wup_ref, cw_ref, cb_ref, wdn_ref, gfin_ref,
                out_ref, hext_ref, ua_ref, ub_ref, *, final):
    acc_ref = out_ref.at[0]
    i = pl.program_id(1)
    n_i = pl.num_programs(1)
    tm = hm_ref.shape[1]

    keep_prev = jnp.where(i > 0, 1.0, 0.0).astype(F32)
    keep_next = jnp.where(i < n_i - 1, 1.0, 0.0).astype(F32)
    hext_ref[0:HALO, :] = (hp_ref[0].astype(F32) * keep_prev).astype(BF16)
    hext_ref[HALO:HALO + tm, :] = hm_ref[0]
    hext_ref[HALO + tm:HALO + tm + HALO, :] = (hn_ref[0].astype(F32) * keep_next).astype(BF16)
    acc_ref[...] = jnp.zeros((tm, D_MODEL), F32)

    def up(c, u_ref):
        u = jnp.dot(hext_ref[...], wup_ref[0, c], preferred_element_type=F32)
        for j in range(FF_SLABS):
            u_ref[j] = u[:, LANES * j:LANES * (j + 1)]

    def mix(c, u_ref):
        cw = cw_ref[0, c]
        cb = cb_ref[0, c]
        ys = []
        for j in range(FF_SLABS):
            lanes = slice(LANES * j, LANES * (j + 1))
            ys.append(u_ref[j, HALO - 1:HALO - 1 + tm, :] * cw[0:1, lanes]
                      + u_ref[j, HALO:HALO + tm, :] * cw[1:2, lanes]
                      + u_ref[j, HALO + 1:HALO + 1 + tm, :] * cw[2:3, lanes] + cb[:, lanes])
        half = FF_SLABS // 2
        acts = []
        for j in range(half):
            gate = ys[half + j]
            acts.append((gate * (1.0 / (1.0 + jnp.exp(-gate)))) * ys[j])
        act = jnp.concatenate(acts, axis=1).astype(BF16)
        acc_ref[...] += jnp.dot(act, wdn_ref[0, c], preferred_element_type=F32)

    def pair(k, carry):
        c = 2 * k
        up(c + 1, ub_ref)
        mix(c, ua_ref)
        up(c + 2, ua_ref)
        mix(c + 1, ub_ref)
        return carry

    assert N_FF_CHUNKS % 2 == 1
    up(0, ua_ref)
    lax.fori_loop(0, N_FF_CHUNKS // 2, pair, 0)
    mix(N_FF_CHUNKS - 1, ua_ref)

    gate_f = mod_ref[0][:, 5 * D_MODEL:6 * D_MODEL]
    x2 = x1_ref[0] + gate_f * acc_ref[...]
    if final:
        r = _rsqrt_mean(jnp.sum(x2 * x2, axis=-1, keepdims=True), D_MODEL)
        x2 = (x2 * r) * gfin_ref[...]
    out_ref[0] = x2


def _ffn(h2, x1, mod, l, w, g_final, tm, final, b0=0, nb=None):
    nb = x1.shape[0] if nb is None else nb
    s = x1.shape[1]
    n_i = s // tm
    hb = tm // HALO
    tok = pl.BlockSpec((1, tm, D_MODEL), lambda b, i: (b + b0, i, 0))
    resident = lambda *blk: pl.BlockSpec((1,) + blk, lambda b, i: (l,) + (0,) * len(blk),
                                         pipeline_mode=pl.Buffered(1))
    u_buf = pltpu.VMEM((FF_SLABS, tm + 2 * HALO, LANES), F32)
    return pl.pallas_call(
        functools.partial(_ffn_kernel, final=final),
        grid=(nb, n_i),
        in_specs=[
            tok,
            pl.BlockSpec((1, HALO, D_MODEL), lambda b, i: (b + b0, jnp.maximum(i * hb - 1, 0), 0)),
            pl.BlockSpec((1, HALO, D_MODEL),
                         lambda b, i: (b + b0, jnp.minimum((i + 1) * hb, s // HALO - 1), 0)),
            tok,
            pl.BlockSpec((1, 1, N_MOD * D_MODEL), lambda b, i: (b + b0, 0, 0)),
            resident(N_FF_CHUNKS, D_MODEL, 2 * FF_CHUNK),
            resident(N_FF_CHUNKS, 3, 2 * FF_CHUNK),
            resident(N_FF_CHUNKS, 1, 2 * FF_CHUNK),
            resident(N_FF_CHUNKS, FF_CHUNK, D_MODEL),
            pl.BlockSpec((1, D_MODEL), lambda b, i: (0, 0)),
        ],
        out_specs=pl.BlockSpec((1, tm, D_MODEL), lambda b, i: (b, i, 0)),
        out_shape=jax.ShapeDtypeStruct((nb, s, D_MODEL), F32),
        scratch_shapes=[pltpu.VMEM((tm + 2 * HALO, D_MODEL), BF16), u_buf, u_buf],
        compiler_params=pltpu.CompilerParams(
            dimension_semantics=("arbitrary", "arbitrary"), vmem_limit_bytes=VMEM_LIMIT),
        name="ffn",
    )(h2, h2, h2, x1, mod, w["w_up"], w["conv_w"], w["conv_b"], w["w_down"], g_final)


def _rope_tables_t(n_tokens, rot_dim):
    rows = n_tokens // GRID_W
    row = jnp.repeat(jnp.arange(rows, dtype=F32), GRID_W)
    col = jnp.tile(jnp.arange(GRID_W, dtype=F32), rows)
    quarter = rot_dim // 4
    inv_freq = ROPE_THETA ** (-jnp.arange(quarter, dtype=F32) / quarter)
    ang = jnp.stack([row, col], axis=-1)[:, :, None] * inv_freq
    cos, sin = jnp.cos(ang), jnp.sin(ang)
    c = jnp.concatenate([cos[:, 0], cos[:, 0], cos[:, 1], cos[:, 1]], axis=-1).T
    s = jnp.concatenate([-sin[:, 0], sin[:, 0], -sin[:, 1], sin[:, 1]], axis=-1).T
    return c, s


def _prepare_weights(g_attn, w_in, g_q_a, g_k_a, g_cq, w_uq, g_ckv, w_ukv, g_out_a, g_out_b, w_o,
                     g_ffn, w_up, conv_w, conv_b, w_down, tm):
    o_kr = D_IN - QK_ROPE_DIM
    w_in_p = jnp.concatenate([
        w_in[..., :o_kr], jnp.zeros((DEPTH, D_MODEL, ROPE_SLOT_OFF), F32), w_in[..., o_kr:],
        jnp.zeros((DEPTH, D_MODEL, LANES - ROPE_SLOT_OFF - QK_ROPE_DIM), F32)], axis=-1).astype(BF16)
    hq = QK_NOPE_DIM + QK_ROPE_DIM
    w_uq_p = jnp.pad(w_uq.reshape(DEPTH, Q_LORA, N_HEADS_B, hq), ((0, 0), (0, 0), (0, 0), (0, HEAD_SLOT - hq)))
    w_uq_t = w_uq_p.reshape(DEPTH, Q_LORA, N_HEADS_B * HEAD_SLOT).transpose(0, 2, 1).astype(BF16)
    w_ukv_h = w_ukv.reshape(DEPTH, KV_LORA, N_HEADS_B, QK_NOPE_DIM + V_DIM_B)
    w_uk = jnp.pad(w_ukv_h[..., :QK_NOPE_DIM], ((0, 0), (0, 0), (0, 0), (0, HEAD_SLOT - QK_NOPE_DIM)))
    w_uk = w_uk.reshape(DEPTH, KV_LORA, N_HEADS_B * HEAD_SLOT).astype(BF16)
    w_uv_t = w_ukv_h[..., QK_NOPE_DIM:].reshape(DEPTH, KV_LORA, WIDTH_B).transpose(0, 2, 1).astype(BF16)
    def chunked(a):
        cols = lambda c, off: a[..., off + FF_CHUNK * c:off + FF_CHUNK * (c + 1)]
        return jnp.stack([jnp.concatenate([cols(c, 0), cols(c, D_FF)], axis=-1) for c in range(N_FF_CHUNKS)],
                         axis=1)

    w_up_c = chunked(w_up.astype(BF16))
    conv_w_c = chunked(conv_w)
    conv_b_c = chunked(conv_b[:, None, :])
    w_down_c = w_down.reshape(DEPTH, N_FF_CHUNKS, FF_CHUNK, D_MODEL).astype(BF16)
    col = lambda g: jnp.broadcast_to(g[:, :, None], g.shape + (tm,))
    return {
        "g_attn": g_attn.reshape(DEPTH, 1, D_MODEL), "w_in": w_in_p,
        "gq_t": col(g_q_a), "gk_t": col(g_k_a), "gcq_t": col(g_cq), "w_uq_t": w_uq_t,
        "g_ckv": g_ckv.reshape(DEPTH, 1, KV_LORA), "w_uk": w_uk, "w_uv_t": w_uv_t,
        "g_out": jnp.concatenate([g_out_a, g_out_b], axis=-1).reshape(DEPTH, 1, D_MIX),
        "w_o": w_o.astype(BF16), "g_ffn": g_ffn.reshape(DEPTH, 1, D_MODEL),
        "w_up": w_up_c, "conv_w": conv_w_c, "conv_b": conv_b_c, "w_down": w_down_c,
    }


def _tiles(s):
    return min(1024, s), min(2048, s), min(1024, s)


def kernel(x_prompt, x_sample, c_prompt, c_sample, w_ada, b_ada, g_attn, w_in, g_q_a, g_k_a, g_cq, w_uq,
           g_ckv, w_ukv, g_out_a, g_out_b, w_o, g_ffn, w_up, conv_w, conv_b, w_down, g_final):
    n_prompt = x_prompt.shape[0]
    s = x_prompt.shape[1]
    assert x_sample.shape[1] == s and s % GRID_W == 0
    tm, tq, tf = _tiles(s)
    assert all(s % t == 0 for t in (tm, tq, tf)) and tf % HALO == 0
    xs = (x_prompt, x_sample)
    c = jnp.concatenate([c_prompt, c_sample], axis=0)
    nb = c.shape[0]

    w = _prepare_weights(g_attn, w_in, g_q_a, g_k_a, g_cq, w_uq, g_ckv, w_ukv, g_out_a, g_out_b, w_o,
                         g_ffn, w_up, conv_w, conv_b, w_down, tm)
    ca, sa = _rope_tables_t(s, HEAD_DIM_A)
    cb, sb = _rope_tables_t(s, QK_ROPE_DIM)
    tabs = {"ca": ca, "sa": sa, "cb": cb, "sb": sb}
    mod_all = _modulation(c, w_ada, b_ada)
    g_fin = g_final.reshape(1, D_MODEL)

    for l in range(DEPTH):
        mod = mod_all[l].reshape(nb, 1, N_MOD * D_MODEL)
        q_all, k_all, v_all = _attn_in(xs, mod, l, w, tabs, tm)
        o_t = _attention(q_all, k_all, v_all, tq)
        x1, h2 = _attn_out(o_t, xs, mod, l, w, tm)
        if l < DEPTH - 1:
            xs = (_ffn(h2, x1, mod, l, w, g_fin, tf, final=False),)
    return (_ffn(h2, x1, mod, DEPTH - 1, w, g_fin, tf, final=True, b0=0, nb=n_prompt),
            _ffn(h2, x1, mod, DEPTH - 1, w, g_fin, tf, final=True, b0=n_prompt, nb=nb - n_prompt))
```

```python
import functools
import math

import jax
import jax.numpy as jnp
from jax import lax
from jax.experimental import pallas as pl
from jax.experimental.pallas import tpu as pltpu

F32 = jnp.float32
BF16 = jnp.bfloat16

D_MODEL = 1024
DEPTH = 4
GRID_W = 64
ROPE_THETA = 10000.0
EPS = 1e-6
N_HEADS_A = 8
N_KV_HEADS_A = 2
HEAD_DIM_A = 64
N_HEADS_B = 8
QK_NOPE_DIM = 64
QK_ROPE_DIM = 32
V_DIM_B = 64
Q_LORA = 256
KV_LORA = 128
WIDTH_A = N_HEADS_A * HEAD_DIM_A
WIDTH_B = N_HEADS_B * V_DIM_B
D_MIX = WIDTH_A + WIDTH_B
KV_WIDTH_A = N_KV_HEADS_A * HEAD_DIM_A
D_IN = WIDTH_A + 2 * KV_WIDTH_A + Q_LORA + KV_LORA + QK_ROPE_DIM
D_FF = 2816
N_MOD = 6

LANES = 128
HEAD_SLOT = 128
D_IN_PAD = 1280
ROPE_SLOT_OFF = 64
FF_CHUNK = 256
N_FF_CHUNKS = D_FF // FF_CHUNK
FF_SLABS = 2 * FF_CHUNK // LANES
HALO = 16
N_HEADS = N_HEADS_A + N_HEADS_B
N_KEY_SLABS = 1 + N_HEADS_B
HEAD_DV = HEAD_DIM_A
assert HEAD_DV == V_DIM_B
V_ROWS = (N_KV_HEADS_A + N_HEADS_B) * HEAD_DV
QUERY_ITEM = 512
KEY_CHUNK = 256
SAFE_LOG2 = 40
FOLD_ROWS = 8
SUB_ROWS = 256
LOG2E = math.log2(math.e)
VMEM_LIMIT = 56 * 1024 * 1024


def _rsqrt_mean(ss, n):
    return lax.rsqrt(ss * (1.0 / n) + EPS)


def _mod_kernel(c_ref, w_ref, b_ref, o_ref):
    c = c_ref[...]
    ca = c * (1.0 / (1.0 + jnp.exp(-c)))
    o_ref[0] = jnp.dot(ca.astype(BF16), w_ref[0].astype(BF16), preferred_element_type=F32) + b_ref[0]


def _modulation(c, w_ada, b_ada):
    nb = c.shape[0]
    n_col = N_MOD * D_MODEL
    tn = D_MODEL
    return pl.pallas_call(
        _mod_kernel,
        grid=(DEPTH, n_col // tn),
        in_specs=[
            pl.BlockSpec((nb, D_MODEL), lambda l, j: (0, 0)),
            pl.BlockSpec((1, D_MODEL, tn), lambda l, j: (l, 0, j)),
            pl.BlockSpec((1, 1, tn), lambda l, j: (l, 0, j)),
        ],
        out_specs=pl.BlockSpec((1, nb, tn), lambda l, j: (l, 0, j)),
        out_shape=jax.ShapeDtypeStruct((DEPTH, nb, n_col), F32),
        compiler_params=pltpu.CompilerParams(dimension_semantics=("arbitrary", "arbitrary")),
        name="modulation",
    )(c, w_ada, b_ada.reshape(DEPTH, 1, n_col))


def _rope_t(x, c, s, q):
    sw = jnp.concatenate([x[q:2 * q], x[0:q], x[3 * q:4 * q], x[2 * q:3 * q]], axis=0)
    return x * c + sw * s


def _token_specs(xs, tm, n_i):
    if len(xs) == 1:
        return [pl.BlockSpec((1, tm, D_MODEL), lambda b, i: (b, i, 0))]
    n_p = xs[0].shape[0]
    return [pl.BlockSpec((1, tm, D_MODEL),
                         lambda b, i: (jnp.minimum(b, n_p - 1), jnp.where(b < n_p, i, n_i - 1), 0)),
            pl.BlockSpec((1, tm, D_MODEL),
                         lambda b, i: (jnp.maximum(b - n_p, 0), jnp.where(b >= n_p, i, 0), 0))]


def _load_tokens(x_refs, n_prompt):
    if len(x_refs) == 1:
        return x_refs[0][0]
    return jnp.where(pl.program_id(0) < n_prompt, x_refs[0][0], x_refs[1][0])


def _attn_in_kernel(*refs, n_x, n_prompt):
    x_refs = refs[:n_x]
    (mod_ref, gat_ref, win_ref, gq_ref, gk_ref, gcq_ref, wuqt_ref, gckv_ref, wuk_ref, wuvt_ref,
     ca_ref, sa_ref, cb_ref, sb_ref, q_ref, k_ref, v_ref) = refs[n_x:]
    x_all = _load_tokens(x_refs, n_prompt)
    tm_all = x_all.shape[0]
    tm = min(SUB_ROWS, tm_all)
    mod = mod_ref[0]
    shift = mod[:, 0:D_MODEL]
    scale = mod[:, D_MODEL:2 * D_MODEL]

    def project(t):
        x = x_all[tm * t:tm * (t + 1)]
        r = _rsqrt_mean(jnp.sum(x * x, axis=-1, keepdims=True), D_MODEL)
        h = (x * r) * gat_ref[0] * (1.0 + scale) + shift
        return jnp.dot(h.astype(BF16), win_ref[0], preferred_element_type=F32)

    def finish(t, proj):
        tok = slice(tm * t, tm * (t + 1))
        ca = ca_ref[:, tok]
        sa = sa_ref[:, tok]
        cb = cb_ref[:, tok]
        sb = sb_ref[:, tok]
        zeros_half = jnp.zeros((HEAD_DIM_A, tm), F32)

        o_qa = 0
        q_at = proj[:, o_qa:o_qa + WIDTH_A].T
        qscale_a = (HEAD_DIM_A ** -0.5) * LOG2E
        gq = gq_ref[0, :, tok]
        for j in range(N_HEADS_A):
            xh = q_at[HEAD_DIM_A * j:HEAD_DIM_A * (j + 1)]
            rh = _rsqrt_mean(jnp.sum(xh * xh, axis=0, keepdims=True), HEAD_DIM_A)
            xr = _rope_t((xh * rh) * gq, ca, sa, HEAD_DIM_A // 4) * qscale_a
            g = j // (N_HEADS_A // N_KV_HEADS_A)
            slot = jnp.concatenate([xr, zeros_half] if g == 0 else [zeros_half, xr], axis=0)
            q_ref[0, HEAD_SLOT * j:HEAD_SLOT * (j + 1), tok] = slot.astype(BF16)

        o_ka = WIDTH_A
        k_at = proj[:, o_ka:o_ka + KV_WIDTH_A].T
        gk = gk_ref[0, :, tok]
        kparts = []
        for g in range(N_KV_HEADS_A):
            xh = k_at[HEAD_DIM_A * g:HEAD_DIM_A * (g + 1)]
            rh = _rsqrt_mean(jnp.sum(xh * xh, axis=0, keepdims=True), HEAD_DIM_A)
            kparts.append(_rope_t((xh * rh) * gk, ca, sa, HEAD_DIM_A // 4))
        k_ref[0, 0, tok, :] = jnp.concatenate(kparts, axis=0).T.astype(BF16)
        o_va = o_ka + KV_WIDTH_A
        v_ref[0, 0:KV_WIDTH_A, tok] = proj[:, o_va:o_va + KV_WIDTH_A].T.astype(BF16)

        o_cq = o_va + KV_WIDTH_A
        cq_t = proj[:, o_cq:o_cq + Q_LORA].T
        rq = _rsqrt_mean(jnp.sum(cq_t * cq_t, axis=0, keepdims=True), Q_LORA)
        cqn = (cq_t * rq) * gcq_ref[0, :, tok]
        qb_t = jnp.dot(wuqt_ref[0], cqn.astype(BF16), preferred_element_type=F32)
        qscale_b = ((QK_NOPE_DIM + QK_ROPE_DIM) ** -0.5) * LOG2E
        zeros_pad = jnp.zeros((HEAD_SLOT - QK_NOPE_DIM - QK_ROPE_DIM, tm), F32)
        for j in range(N_HEADS_B):
            base = HEAD_SLOT * j
            nope = qb_t[base:base + QK_NOPE_DIM]
            rope = _rope_t(qb_t[base + ROPE_SLOT_OFF:base + ROPE_SLOT_OFF + QK_ROPE_DIM], cb, sb,
                           QK_ROPE_DIM // 4)
            slot = jnp.concatenate([nope, rope, zeros_pad], axis=0) * qscale_b
            row0 = N_HEADS_A * HEAD_SLOT + base
            q_ref[0, row0:row0 + HEAD_SLOT, tok] = slot.astype(BF16)

        o_ckv = o_cq + Q_LORA
        ckv = proj[:, o_ckv:o_ckv + KV_LORA]
        rkv = _rsqrt_mean(jnp.sum(ckv * ckv, axis=-1, keepdims=True), KV_LORA)
        ckvn = (ckv * rkv) * gckv_ref[0]
        k_nope = jnp.dot(ckvn.astype(BF16), wuk_ref[0], preferred_element_type=F32)
        v_ref[0, KV_WIDTH_A:V_ROWS, tok] = jnp.dot(
            wuvt_ref[0], ckvn.T.astype(BF16), preferred_element_type=F32).astype(BF16)
        o_kr = o_ckv + KV_LORA
        kr_t = proj[:, o_kr:o_kr + LANES].T
        kr_rope = _rope_t(kr_t[ROPE_SLOT_OFF:ROPE_SLOT_OFF + QK_ROPE_DIM], cb, sb, QK_ROPE_DIM // 4)
        kr_tile = jnp.concatenate(
            [jnp.zeros((ROPE_SLOT_OFF, tm), F32), kr_rope, zeros_pad], axis=0).T
        for j in range(N_HEADS_B):
            base = HEAD_SLOT * j
            k_ref[0, 1 + j, tok, :] = (k_nope[:, base:base + HEAD_SLOT] + kr_tile).astype(BF16)

    n_sub = tm_all // tm
    projs = [project(t) for t in range(n_sub)]
    for t in range(n_sub):
        finish(t, projs[t])


def _attn_in(xs, mod, l, w, tabs, tm):
    nb = sum(x.shape[0] for x in xs)
    s = xs[0].shape[1]
    grid = (nb, s // tm)
    lay = lambda *blk: pl.BlockSpec((1,) + blk, lambda b, i: (l,) + (0,) * len(blk))
    tab = lambda rows: pl.BlockSpec((rows, tm), lambda b, i: (0, i))
    out_shapes = (
        jax.ShapeDtypeStruct((nb, N_HEADS * HEAD_SLOT, s), BF16),
        jax.ShapeDtypeStruct((nb, N_KEY_SLABS, s, HEAD_SLOT), BF16),
        jax.ShapeDtypeStruct((nb, V_ROWS, s), BF16),
    )
    return pl.pallas_call(
        functools.partial(_attn_in_kernel, n_x=len(xs), n_prompt=xs[0].shape[0]),
        grid=grid,
        in_specs=_token_specs(xs, tm, s // tm) + [
            pl.BlockSpec((1, 1, N_MOD * D_MODEL), lambda b, i: (b, 0, 0)),
            lay(1, D_MODEL),
            lay(D_MODEL, D_IN_PAD),
            lay(HEAD_DIM_A, tm),
            lay(HEAD_DIM_A, tm),
            lay(Q_LORA, tm),
            lay(N_HEADS_B * HEAD_SLOT, Q_LORA),
            lay(1, KV_LORA),
            lay(KV_LORA, N_HEADS_B * HEAD_SLOT),
            lay(WIDTH_B, KV_LORA),
            tab(HEAD_DIM_A), tab(HEAD_DIM_A), tab(QK_ROPE_DIM), tab(QK_ROPE_DIM),
        ],
        out_specs=(pl.BlockSpec((1, N_HEADS * HEAD_SLOT, tm), lambda b, i: (b, 0, i)),
                   pl.BlockSpec((1, N_KEY_SLABS, tm, HEAD_SLOT), lambda b, i: (b, 0, i, 0)),
                   pl.BlockSpec((1, V_ROWS, tm), lambda b, i: (b, 0, i))),
        out_shape=out_shapes,
        compiler_params=pltpu.CompilerParams(
            dimension_semantics=("arbitrary", "arbitrary"), vmem_limit_bytes=VMEM_LIMIT),
        name="attn_in",
    )(*xs, mod, w["g_attn"], w["w_in"], w["gq_t"], w["gk_t"], w["gcq_t"], w["w_uq_t"], w["g_ckv"],
      w["w_uk"], w["w_uv_t"], tabs["ca"], tabs["sa"], tabs["cb"], tabs["sb"])


def _fold_rows(x, op):
    parts = [x[FOLD_ROWS * i:FOLD_ROWS * (i + 1)] for i in range(x.shape[0] // FOLD_ROWS)]
    while len(parts) > 1:
        parts = [op(parts[2 * i], parts[2 * i + 1]) for i in range(len(parts) // 2)]
    return parts[0]


def _key_slab(h):
    if isinstance(h, int):
        return 0 if h < N_HEADS_A else h - N_HEADS_A + 1
    return jnp.where(h < N_HEADS_A, 0, h - N_HEADS_A + 1)


def _value_row(h):
    per_kv = N_HEADS_A // N_KV_HEADS_A
    if isinstance(h, int):
        return HEAD_DV * (h // per_kv if h < N_HEADS_A else h - N_HEADS_A + N_KV_HEADS_A)
    return HEAD_DV * jnp.where(h < N_HEADS_A, h // per_kv, h - N_HEADS_A + N_KV_HEADS_A)


def _row_block(start, size):
    if isinstance(start, int):
        return pl.ds(start, size)
    return pl.ds(pl.multiple_of(start, size), size)


def _attention_kernel(q_ref, k_ref, v_ref, ot_ref, s0_ref, s1_ref):
    n_keys = k_ref.shape[2]
    w = s0_ref.shape[1]
    n_parts = ot_ref.shape[2] // w
    n_chunks = n_keys // KEY_CHUNK
    rows = lambda c: slice(KEY_CHUNK * c, KEY_CHUNK * (c + 1))
    cols = lambda t: slice(w * t, w * (t + 1))

    def stage(nxt, s_next_ref, cur, s_cur_ref, carry, fast):
        m_next = None
        if nxt is not None:
            h_next, t_next = nxt
            q_t = q_ref[0, _row_block(h_next * HEAD_SLOT, HEAD_SLOT), cols(t_next)]
            slab = _key_slab(h_next)
        if cur is not None:
            h_cur, t_cur = cur
            if not fast:
                m_b = jnp.broadcast_to(jnp.max(carry, axis=0, keepdims=True), (KEY_CHUNK, w))
            v_rows = _row_block(_value_row(h_cur), HEAD_DV)
            l_acc = jnp.zeros((FOLD_ROWS, w), F32)
            o_acc = jnp.zeros((HEAD_DV, w), F32)
        for c in range(n_chunks):
            if nxt is not None:
                s = jnp.dot(k_ref[0, slab, rows(c), :], q_t, preferred_element_type=F32)
                s_next_ref[rows(c), :] = s
                if not fast:
                    folded = _fold_rows(s, jnp.maximum)
                    m_next = folded if m_next is None else jnp.maximum(m_next, folded)
            if cur is not None:
                s_cur = s_cur_ref[rows(c), :]
                p = jnp.exp2(s_cur if fast else s_cur - m_b)
                l_acc = l_acc + _fold_rows(p, jnp.add)
                o_acc = o_acc + jnp.dot(v_ref[0, v_rows, rows(c)], p.astype(BF16),
                                        preferred_element_type=F32)
        if cur is not None:
            denom = jnp.sum(l_acc, axis=0, keepdims=True)
            ot_ref[0, _row_block(h_cur * HEAD_DV, HEAD_DV), cols(t_cur)] = (
                o_acc * (1.0 / denom)).astype(ot_ref.dtype)
            if fast:
                carry = (jnp.minimum(carry[0], denom), jnp.maximum(carry[1], denom))
        return carry if fast else m_next

    bufs = (s0_ref, s1_ref)

    def run(fast):
        def head(h, carry, last=False):
            for t in range(n_parts):
                nxt = (h, t + 1) if t + 1 < n_parts else (None if last else (h + 1, 0))
                carry = stage(nxt, bufs[(t + 1) % 2] if nxt is not None else None, (h, t), bufs[t % 2],
                              carry, fast)
            return carry

        init = (jnp.ones((1, w), F32), jnp.ones((1, w), F32)) if fast else None
        carry = lax.fori_loop(0, N_HEADS - 1, head, stage((0, 0), s0_ref, None, None, init, fast))
        return head(N_HEADS - 1, carry, last=True)

    d_min, d_max = run(fast=True)
    in_range = jnp.logical_and(d_min >= 2.0 ** -SAFE_LOG2, d_max <= 2.0 ** SAFE_LOG2)
    redo = jnp.min(jnp.where(in_range, 1, 0)) == 0

    @pl.when(redo)
    def _():
        run(fast=False)


def _attention(q_all, k_all, v_all, tq):
    nb, _, s, _ = k_all.shape
    w = min(QUERY_ITEM, tq // 2)
    assert s % KEY_CHUNK == 0 and w % LANES == 0 and tq % (2 * w) == 0
    score_buf = pltpu.VMEM((s, w), F32)
    return pl.pallas_call(
        _attention_kernel,
        grid=(nb, s // tq),
        in_specs=[pl.BlockSpec((1, N_HEADS * HEAD_SLOT, tq), lambda b, i: (b, 0, i)),
                  pl.BlockSpec((1, N_KEY_SLABS, s, HEAD_SLOT), lambda b, i: (b, 0, 0, 0)),
                  pl.BlockSpec((1, V_ROWS, s), lambda b, i: (b, 0, 0))],
        out_specs=pl.BlockSpec((1, D_MIX, tq), lambda b, i: (b, 0, i)),
        out_shape=jax.ShapeDtypeStruct((nb, D_MIX, s), BF16),
        scratch_shapes=[score_buf, score_buf],
        compiler_params=pltpu.CompilerParams(
            dimension_semantics=("arbitrary", "arbitrary"), vmem_limit_bytes=VMEM_LIMIT),
        name="attention",
    )(q_all, k_all, v_all)


def _attn_out_kernel(ot_ref, *refs, n_x, n_prompt):
    x_refs = refs[:n_x]
    mod_ref, go_ref, wo_ref, gffn_ref, x1_ref, h2_ref = refs[n_x:]
    o_t = ot_ref[0].astype(F32)
    oa = o_t[:WIDTH_A]
    ob = o_t[WIDTH_A:]
    ra = _rsqrt_mean(jnp.sum(oa * oa, axis=0, keepdims=True), WIDTH_A)
    rb = _rsqrt_mean(jnp.sum(ob * ob, axis=0, keepdims=True), WIDTH_B)
    on = jnp.concatenate([oa * ra, ob * rb], axis=0).T * go_ref[0]
    res = jnp.dot(on.astype(BF16), wo_ref[0], preferred_element_type=F32)
    mod = mod_ref[0]
    gate_a = mod[:, 2 * D_MODEL:3 * D_MODEL]
    shift_f = mod[:, 3 * D_MODEL:4 * D_MODEL]
    scale_f = mod[:, 4 * D_MODEL:5 * D_MODEL]
    x1 = _load_tokens(x_refs, n_prompt) + gate_a * res
    x1_ref[0] = x1
    r = _rsqrt_mean(jnp.sum(x1 * x1, axis=-1, keepdims=True), D_MODEL)
    h2_ref[0] = ((x1 * r) * gffn_ref[0] * (1.0 + scale_f) + shift_f).astype(BF16)


def _attn_out(o_t, xs, mod, l, w, tm):
    nb, _, s = o_t.shape
    lay = lambda *blk: pl.BlockSpec((1,) + blk, lambda b, i: (l,) + (0,) * len(blk))
    tok = pl.BlockSpec((1, tm, D_MODEL), lambda b, i: (b, i, 0))
    return pl.pallas_call(
        functools.partial(_attn_out_kernel, n_x=len(xs), n_prompt=xs[0].shape[0]),
        grid=(nb, s // tm),
        in_specs=[pl.BlockSpec((1, D_MIX, tm), lambda b, i: (b, 0, i))] + _token_specs(xs, tm, s // tm) + [
            pl.BlockSpec((1, 1, N_MOD * D_MODEL), lambda b, i: (b, 0, 0)),
            lay(1, D_MIX),
            lay(D_MIX, D_MODEL),
            lay(1, D_MODEL),
        ],
        out_specs=(tok, tok),
        out_shape=(jax.ShapeDtypeStruct((nb, s, D_MODEL), F32), jax.ShapeDtypeStruct((nb, s, D_MODEL), BF16)),
        compiler_params=pltpu.CompilerParams(
            dimension_semantics=("arbitrary", "arbitrary"), vmem_limit_bytes=VMEM_LIMIT),
        name="attn_out",
    )(o_t, *xs, mod, w["g_out"], w["w_o"], w["g_ffn"])


def _ffn_kernel(hm_ref, hp_ref, hn_ref, x1_ref, mod_ref, wup_ref, cw_ref, cb_ref, wdn_ref, gfin_ref,
                out_ref, hext_ref, ua_ref, ub_ref, *, final):
    acc_ref = out_ref.at[0]
    i = pl.program_id(1)
    n_i = pl.num_programs(1)
    tm = hm_ref.shape[1]

    keep_prev = jnp.where(i > 0, 1.0, 0.0).astype(F32)
    keep_next = jnp.where(i < n_i - 1, 1.0, 0.0).astype(F32)
    hext_ref[0:HALO, :] = (hp_ref[0].astype(F32) * keep_prev).astype(BF16)
    hext_ref[HALO:HALO + tm, :] = hm_ref[0]
    hext_ref[HALO + tm:HALO + tm + HALO, :] = (hn_ref[0].astype(F32) * keep_next).astype(BF16)
    acc_ref[...] = jnp.zeros((tm, D_MODEL), F32)

    def up(c, u_ref):
        u = jnp.dot(hext_ref[...], wup_ref[0, c], preferred_element_type=F32)
        for j in range(FF_SLABS):
            u_ref[j] = u[:, LANES * j:LANES * (j + 1)]

    def mix(c, u_ref):
        cw = cw_ref[0, c]
        cb = cb_ref[0, c]
        ys = []
        for j in range(FF_SLABS):
            lanes = slice(LANES * j, LANES * (j + 1))
            ys.append(u_ref[j, HALO - 1:HALO - 1 + tm, :] * cw[0:1, lanes]
                      + u_ref[j, HALO:HALO + tm, :] * cw[1:2, lanes]
                      + u_ref[j, HALO + 1:HALO + 1 + tm, :] * cw[2:3, lanes] + cb[:, lanes])
        half = FF_SLABS // 2
        acts = []
        for j in range(half):
            gate = ys[half + j]
            acts.append((gate * (1.0 / (1.0 + jnp.exp(-gate)))) * ys[j])
        act = jnp.concatenate(acts, axis=1).astype(BF16)
        acc_ref[...] += jnp.dot(act, wdn_ref[0, c], preferred_element_type=F32)

    bufs = (ua_ref, ub_ref)
    up(0, ua_ref)
    for c in range(N_FF_CHUNKS):
        if c + 1 < N_FF_CHUNKS:
            up(c + 1, bufs[(c + 1) % 2])
        mix(c, bufs[c % 2])

    gate_f = mod_ref[0][:, 5 * D_MODEL:6 * D_MODEL]
    x2 = x1_ref[0] + gate_f * acc_ref[...]
    if final:
        r = _rsqrt_mean(jnp.sum(x2 * x2, axis=-1, keepdims=True), D_MODEL)
        x2 = (x2 * r) * gfin_ref[...]
    out_ref[0] = x2


def _ffn(h2, x1, mod, l, w, g_final, tm, final, b0=0, nb=None):
    nb = x1.shape[0] if nb is None else nb
    s = x1.shape[1]
    n_i = s // tm
    hb = tm // HALO
    tok = pl.BlockSpec((1, tm, D_MODEL), lambda b, i: (b + b0, i, 0))
    resident = lambda *blk: pl.BlockSpec((1,) + blk, lambda b, i: (l,) + (0,) * len(blk),
                                         pipeline_mode=pl.Buffered(1))
    u_buf = pltpu.VMEM((FF_SLABS, tm + 2 * HALO, LANES), F32)
    return pl.pallas_call(
        functools.partial(_ffn_kernel, final=final),
        grid=(nb, n_i),
        in_specs=[
            tok,
            pl.BlockSpec((1, HALO, D_MODEL), lambda b, i: (b + b0, jnp.maximum(i * hb - 1, 0), 0)),
            pl.BlockSpec((1, HALO, D_MODEL),
                         lambda b, i: (b + b0, jnp.minimum((i + 1) * hb, s // HALO - 1), 0)),
            tok,
            pl.BlockSpec((1, 1, N_MOD * D_MODEL), lambda b, i: (b + b0, 0, 0)),
            resident(N_FF_CHUNKS, D_MODEL, 2 * FF_CHUNK),
            resident(N_FF_CHUNKS, 3, 2 * FF_CHUNK),
            resident(N_FF_CHUNKS, 1, 2 * FF_CHUNK),
            resident(N_FF_CHUNKS, FF_CHUNK, D_MODEL),
            pl.BlockSpec((1, D_MODEL), lambda b, i: (0, 0)),
        ],
        out_specs=pl.BlockSpec((1, tm, D_MODEL), lambda b, i: (b, i, 0)),
        out_shape=jax.ShapeDtypeStruct((nb, s, D_MODEL), F32),
        scratch_shapes=[pltpu.VMEM((tm + 2 * HALO, D_MODEL), BF16), u_buf, u_buf],
        compiler_params=pltpu.CompilerParams(
            dimension_semantics=("arbitrary", "arbitrary"), vmem_limit_bytes=VMEM_LIMIT),
        name="ffn",
    )(h2, h2, h2, x1, mod, w["w_up"], w["conv_w"], w["conv_b"], w["w_down"], g_final)


def _rope_tables_t(n_tokens, rot_dim):
    rows = n_tokens // GRID_W
    row = jnp.repeat(jnp.arange(rows, dtype=F32), GRID_W)
    col = jnp.tile(jnp.arange(GRID_W, dtype=F32), rows)
    quarter = rot_dim // 4
    inv_freq = ROPE_THETA ** (-jnp.arange(quarter, dtype=F32) / quarter)
    ang = jnp.stack([row, col], axis=-1)[:, :, None] * inv_freq
    cos, sin = jnp.cos(ang), jnp.sin(ang)
    c = jnp.concatenate([cos[:, 0], cos[:, 0], cos[:, 1], cos[:, 1]], axis=-1).T
    s = jnp.concatenate([-sin[:, 0], sin[:, 0], -sin[:, 1], sin[:, 1]], axis=-1).T
    return c, s


def _prepare_weights(g_attn, w_in, g_q_a, g_k_a, g_cq, w_uq, g_ckv, w_ukv, g_out_a, g_out_b, w_o,
                     g_ffn, w_up, conv_w, conv_b, w_down, tm):
    o_kr = D_IN - QK_ROPE_DIM
    w_in_p = jnp.concatenate([
        w_in[..., :o_kr], jnp.zeros((DEPTH, D_MODEL, ROPE_SLOT_OFF), F32), w_in[..., o_kr:],
        jnp.zeros((DEPTH, D_MODEL, LANES - ROPE_SLOT_OFF - QK_ROPE_DIM), F32)], axis=-1).astype(BF16)
    hq = QK_NOPE_DIM + QK_ROPE_DIM
    w_uq_p = jnp.pad(w_uq.reshape(DEPTH, Q_LORA, N_HEADS_B, hq), ((0, 0), (0, 0), (0, 0), (0, HEAD_SLOT - hq)))
    w_uq_t = w_uq_p.reshape(DEPTH, Q_LORA, N_HEADS_B * HEAD_SLOT).transpose(0, 2, 1).astype(BF16)
    w_ukv_h = w_ukv.reshape(DEPTH, KV_LORA, N_HEADS_B, QK_NOPE_DIM + V_DIM_B)
    w_uk = jnp.pad(w_ukv_h[..., :QK_NOPE_DIM], ((0, 0), (0, 0), (0, 0), (0, HEAD_SLOT - QK_NOPE_DIM)))
    w_uk = w_uk.reshape(DEPTH, KV_LORA, N_HEADS_B * HEAD_SLOT).astype(BF16)
    w_uv_t = w_ukv_h[..., QK_NOPE_DIM:].reshape(DEPTH, KV_LORA, WIDTH_B).transpose(0, 2, 1).astype(BF16)
    def chunked(a):
        cols = lambda c, off: a[..., off + FF_CHUNK * c:off + FF_CHUNK * (c + 1)]
        return jnp.stack([jnp.concatenate([cols(c, 0), cols(c, D_FF)], axis=-1) for c in range(N_FF_CHUNKS)],
                         axis=1)

    w_up_c = chunked(w_up.astype(BF16))
    conv_w_c = chunked(conv_w)
    conv_b_c = chunked(conv_b[:, None, :])
    w_down_c = w_down.reshape(DEPTH, N_FF_CHUNKS, FF_CHUNK, D_MODEL).astype(BF16)
    col = lambda g: jnp.broadcast_to(g[:, :, None], g.shape + (tm,))
    return {
        "g_attn": g_attn.reshape(DEPTH, 1, D_MODEL), "w_in": w_in_p,
        "gq_t": col(g_q_a), "gk_t": col(g_k_a), "gcq_t": col(g_cq), "w_uq_t": w_uq_t,
        "g_ckv": g_ckv.reshape(DEPTH, 1, KV_LORA), "w_uk": w_uk, "w_uv_t": w_uv_t,
        "g_out": jnp.concatenate([g_out_a, g_out_b], axis=-1).reshape(DEPTH, 1, D_MIX),
        "w_o": w_o.astype(BF16), "g_ffn": g_ffn.reshape(DEPTH, 1, D_MODEL),
        "w_up": w_up_c, "conv_w": conv_w_c, "conv_b": conv_b_c, "w_down": w_down_c,
    }


def _tiles(s):
    return min(1024, s), min(2048, s), min(1024, s)


def kernel(x_prompt, x_sample, c_prompt, c_sample, w_ada, b_ada, g_attn, w_in, g_q_a, g_k_a, g_cq, w_uq,
           g_ckv, w_ukv, g_out_a, g_out_b, w_o, g_ffn, w_up, conv_w, conv_b, w_down, g_final):
    n_prompt = x_prompt.shape[0]
    s = x_prompt.shape[1]
    assert x_sample.shape[1] == s and s % GRID_W == 0
    tm, tq, tf = _tiles(s)
    assert all(s % t == 0 for t in (tm, tq, tf)) and tf % HALO == 0
    xs = (x_prompt, x_sample)
    c = jnp.concatenate([c_prompt, c_sample], axis=0)
    nb = c.shape[0]

    w = _prepare_weights(g_attn, w_in, g_q_a, g_k_a, g_cq, w_uq, g_ckv, w_ukv, g_out_a, g_out_b, w_o,
                         g_ffn, w_up, conv_w, conv_b, w_down, tm)
    ca, sa = _rope_tables_t(s, HEAD_DIM_A)
    cb, sb = _rope_tables_t(s, QK_ROPE_DIM)
    tabs = {"ca": ca, "sa": sa, "cb": cb, "sb": sb}
    mod_all = _modulation(c, w_ada, b_ada)
    g_fin = g_final.reshape(1, D_MODEL)

    for l in range(DEPTH):
        mod = mod_all[l].reshape(nb, 1, N_MOD * D_MODEL)
        q_all, k_all, v_all = _attn_in(xs, mod, l, w, tabs, tm)
        o_t = _attention(q_all, k_all, v_all, tq)
        x1, h2 = _attn_out(o_t, xs, mod, l, w, tm)
        if l < DEPTH - 1:
            xs = (_ffn(h2, x1, mod, l, w, g_fin, tf, final=False),)
    return (_ffn(h2, x1, mod, DEPTH - 1, w, g_fin, tf, final=True, b0=0, nb=n_prompt),
            _ffn(h2, x1, mod, DEPTH - 1, w, g_fin, tf, final=True, b0=n_prompt, nb=nb - n_prompt))
```

```python
import functools
import math

import jax
import jax.numpy as jnp
from jax import lax
from jax.experimental import pallas as pl
from jax.experimental.pallas import tpu as pltpu

F32 = jnp.float32
BF16 = jnp.bfloat16

D_MODEL = 1024
DEPTH = 4
GRID_W = 64
ROPE_THETA = 10000.0
EPS = 1e-6
N_HEADS_A = 8
N_KV_HEADS_A = 2
HEAD_DIM_A = 64
N_HEADS_B = 8
QK_NOPE_DIM = 64
QK_ROPE_DIM = 32
V_DIM_B = 64
Q_LORA = 256
KV_LORA = 128
WIDTH_A = N_HEADS_A * HEAD_DIM_A
WIDTH_B = N_HEADS_B * V_DIM_B
D_MIX = WIDTH_A + WIDTH_B
KV_WIDTH_A = N_KV_HEADS_A * HEAD_DIM_A
D_IN = WIDTH_A + 2 * KV_WIDTH_A + Q_LORA + KV_LORA + QK_ROPE_DIM
D_FF = 2816
N_MOD = 6

LANES = 128
HEAD_SLOT = 128
D_IN_PAD = 1280
ROPE_SLOT_OFF = 64
FF_CHUNK = 256
N_FF_CHUNKS = D_FF // FF_CHUNK
FF_SLABS = 2 * FF_CHUNK // LANES
HALO = 16
N_HEADS = N_HEADS_A + N_HEADS_B
N_KEY_SLABS = 1 + N_HEADS_B
HEAD_DV = HEAD_DIM_A
assert HEAD_DV == V_DIM_B
V_ROWS = (N_KV_HEADS_A + N_HEADS_B) * HEAD_DV
QUERY_ITEM = 512
KEY_CHUNK = 256
HEADS_PER_ITER = 3
SAFE_LOG2 = 40
FOLD_ROWS = 8
SUB_ROWS = 256
LOG2E = math.log2(math.e)
VMEM_LIMIT = 56 * 1024 * 1024


def _rsqrt_mean(ss, n):
    return lax.rsqrt(ss * (1.0 / n) + EPS)


def _mod_kernel(c_ref, w_ref, b_ref, o_ref):
    c = c_ref[...]
    ca = c * (1.0 / (1.0 + jnp.exp(-c)))
    o_ref[0] = jnp.dot(ca.astype(BF16), w_ref[0].astype(BF16), preferred_element_type=F32) + b_ref[0]


def _modulation(c, w_ada, b_ada):
    nb = c.shape[0]
    n_col = N_MOD * D_MODEL
    tn = D_MODEL
    return pl.pallas_call(
        _mod_kernel,
        grid=(DEPTH, n_col // tn),
        in_specs=[
            pl.BlockSpec((nb, D_MODEL), lambda l, j: (0, 0)),
            pl.BlockSpec((1, D_MODEL, tn), lambda l, j: (l, 0, j)),
            pl.BlockSpec((1, 1, tn), lambda l, j: (l, 0, j)),
        ],
        out_specs=pl.BlockSpec((1, nb, tn), lambda l, j: (l, 0, j)),
        out_shape=jax.ShapeDtypeStruct((DEPTH, nb, n_col), F32),
        compiler_params=pltpu.CompilerParams(dimension_semantics=("arbitrary", "arbitrary")),
        name="modulation",
    )(c, w_ada, b_ada.reshape(DEPTH, 1, n_col))


def _rope_t(x, c, s, q):
    sw = jnp.concatenate([x[q:2 * q], x[0:q], x[3 * q:4 * q], x[2 * q:3 * q]], axis=0)
    return x * c + sw * s


def _token_specs(xs, tm, n_i):
    if len(xs) == 1:
        return [pl.BlockSpec((1, tm, D_MODEL), lambda b, i: (b, i, 0))]
    n_p = xs[0].shape[0]
    return [pl.BlockSpec((1, tm, D_MODEL),
                         lambda b, i: (jnp.minimum(b, n_p - 1), jnp.where(b < n_p, i, n_i - 1), 0)),
            pl.BlockSpec((1, tm, D_MODEL),
                         lambda b, i: (jnp.maximum(b - n_p, 0), jnp.where(b >= n_p, i, 0), 0))]


def _load_tokens(x_refs, n_prompt):
    if len(x_refs) == 1:
        return x_refs[0][0]
    return jnp.where(pl.program_id(0) < n_prompt, x_refs[0][0], x_refs[1][0])


def _attn_in_kernel(*refs, n_x, n_prompt):
    x_refs = refs[:n_x]
    (mod_ref, gat_ref, win_ref, gq_ref, gk_ref, gcq_ref, wuqt_ref, gckv_ref, wuk_ref, wuvt_ref,
     ca_ref, sa_ref, cb_ref, sb_ref, q_ref, k_ref, v_ref) = refs[n_x:]
    x_all = _load_tokens(x_refs, n_prompt)
    tm_all = x_all.shape[0]
    tm = min(SUB_ROWS, tm_all)
    mod = mod_ref[0]
    shift = mod[:, 0:D_MODEL]
    scale = mod[:, D_MODEL:2 * D_MODEL]

    def project(t):
        x = x_all[tm * t:tm * (t + 1)]
        r = _rsqrt_mean(jnp.sum(x * x, axis=-1, keepdims=True), D_MODEL)
        h = (x * r) * gat_ref[0] * (1.0 + scale) + shift
        return jnp.dot(h.astype(BF16), win_ref[0], preferred_element_type=F32)

    def finish(t, proj):
        tok = slice(tm * t, tm * (t + 1))
        ca = ca_ref[:, tok]
        sa = sa_ref[:, tok]
        cb = cb_ref[:, tok]
        sb = sb_ref[:, tok]
        zeros_half = jnp.zeros((HEAD_DIM_A, tm), F32)

        o_qa = 0
        q_at = proj[:, o_qa:o_qa + WIDTH_A].T
        qscale_a = (HEAD_DIM_A ** -0.5) * LOG2E
        gq = gq_ref[0, :, tok]
        for j in range(N_HEADS_A):
            xh = q_at[HEAD_DIM_A * j:HEAD_DIM_A * (j + 1)]
            rh = _rsqrt_mean(jnp.sum(xh * xh, axis=0, keepdims=True), HEAD_DIM_A)
            xr = _rope_t((xh * rh) * gq, ca, sa, HEAD_DIM_A // 4) * qscale_a
            g = j // (N_HEADS_A // N_KV_HEADS_A)
            slot = jnp.concatenate([xr, zeros_half] if g == 0 else [zeros_half, xr], axis=0)
            q_ref[0, HEAD_SLOT * j:HEAD_SLOT * (j + 1), tok] = slot.astype(BF16)

        o_ka = WIDTH_A
        k_at = proj[:, o_ka:o_ka + KV_WIDTH_A].T
        gk = gk_ref[0, :, tok]
        kparts = []
        for g in range(N_KV_HEADS_A):
            xh = k_at[HEAD_DIM_A * g:HEAD_DIM_A * (g + 1)]
            rh = _rsqrt_mean(jnp.sum(xh * xh, axis=0, keepdims=True), HEAD_DIM_A)
            kparts.append(_rope_t((xh * rh) * gk, ca, sa, HEAD_DIM_A // 4))
        k_ref[0, 0, tok, :] = jnp.concatenate(kparts, axis=0).T.astype(BF16)
        o_va = o_ka + KV_WIDTH_A
        v_ref[0, 0:KV_WIDTH_A, tok] = proj[:, o_va:o_va + KV_WIDTH_A].T.astype(BF16)

        o_cq = o_va + KV_WIDTH_A
        cq_t = proj[:, o_cq:o_cq + Q_LORA].T
        rq = _rsqrt_mean(jnp.sum(cq_t * cq_t, axis=0, keepdims=True), Q_LORA)
        cqn = (cq_t * rq) * gcq_ref[0, :, tok]
        qb_t = jnp.dot(wuqt_ref[0], cqn.astype(BF16), preferred_element_type=F32)
        qscale_b = ((QK_NOPE_DIM + QK_ROPE_DIM) ** -0.5) * LOG2E
        zeros_pad = jnp.zeros((HEAD_SLOT - QK_NOPE_DIM - QK_ROPE_DIM, tm), F32)
        for j in range(N_HEADS_B):
            base = HEAD_SLOT * j
            nope = qb_t[base:base + QK_NOPE_DIM]
            rope = _rope_t(qb_t[base + ROPE_SLOT_OFF:base + ROPE_SLOT_OFF + QK_ROPE_DIM], cb, sb,
                           QK_ROPE_DIM // 4)
            slot = jnp.concatenate([nope, rope, zeros_pad], axis=0) * qscale_b
            row0 = N_HEADS_A * HEAD_SLOT + base
            q_ref[0, row0:row0 + HEAD_SLOT, tok] = slot.astype(BF16)

        o_ckv = o_cq + Q_LORA
        ckv = proj[:, o_ckv:o_ckv + KV_LORA]
        rkv = _rsqrt_mean(jnp.sum(ckv * ckv, axis=-1, keepdims=True), KV_LORA)
        ckvn = (ckv * rkv) * gckv_ref[0]
        k_nope = jnp.dot(ckvn.astype(BF16), wuk_ref[0], preferred_element_type=F32)
        v_ref[0, KV_WIDTH_A:V_ROWS, tok] = jnp.dot(
            wuvt_ref[0], ckvn.T.astype(BF16), preferred_element_type=F32).astype(BF16)
        o_kr = o_ckv + KV_LORA
        kr_t = proj[:, o_kr:o_kr + LANES].T
        kr_rope = _rope_t(kr_t[ROPE_SLOT_OFF:ROPE_SLOT_OFF + QK_ROPE_DIM], cb, sb, QK_ROPE_DIM // 4)
        kr_tile = jnp.concatenate(
            [jnp.zeros((ROPE_SLOT_OFF, tm), F32), kr_rope, zeros_pad], axis=0).T
        for j in range(N_HEADS_B):
            base = HEAD_SLOT * j
            k_ref[0, 1 + j, tok, :] = (k_nope[:, base:base + HEAD_SLOT] + kr_tile).astype(BF16)

    n_sub = tm_all // tm
    projs = [project(t) for t in range(n_sub)]
    for t in range(n_sub):
        finish(t, projs[t])


def _attn_in(xs, mod, l, w, tabs, tm):
    nb = sum(x.shape[0] for x in xs)
    s = xs[0].shape[1]
    grid = (nb, s // tm)
    lay = lambda *blk: pl.BlockSpec((1,) + blk, lambda b, i: (l,) + (0,) * len(blk))
    tab = lambda rows: pl.BlockSpec((rows, tm), lambda b, i: (0, i))
    out_shapes = (
        jax.ShapeDtypeStruct((nb, N_HEADS * HEAD_SLOT, s), BF16),
        jax.ShapeDtypeStruct((nb, N_KEY_SLABS, s, HEAD_SLOT), BF16),
        jax.ShapeDtypeStruct((nb, V_ROWS, s), BF16),
    )
    return pl.pallas_call(
        functools.partial(_attn_in_kernel, n_x=len(xs), n_prompt=xs[0].shape[0]),
        grid=grid,
        in_specs=_token_specs(xs, tm, s // tm) + [
            pl.BlockSpec((1, 1, N_MOD * D_MODEL), lambda b, i: (b, 0, 0)),
            lay(1, D_MODEL),
            lay(D_MODEL, D_IN_PAD),
            lay(HEAD_DIM_A, tm),
            lay(HEAD_DIM_A, tm),
            lay(Q_LORA, tm),
            lay(N_HEADS_B * HEAD_SLOT, Q_LORA),
            lay(1, KV_LORA),
            lay(KV_LORA, N_HEADS_B * HEAD_SLOT),
            lay(WIDTH_B, KV_LORA),
            tab(HEAD_DIM_A), tab(HEAD_DIM_A), tab(QK_ROPE_DIM), tab(QK_ROPE_DIM),
        ],
        out_specs=(pl.BlockSpec((1, N_HEADS * HEAD_SLOT, tm), lambda b, i: (b, 0, i)),
                   pl.BlockSpec((1, N_KEY_SLABS, tm, HEAD_SLOT), lambda b, i: (b, 0, i, 0)),
                   pl.BlockSpec((1, V_ROWS, tm), lambda b, i: (b, 0, i))),
        out_shape=out_shapes,
        compiler_params=pltpu.CompilerParams(
            dimension_semantics=("arbitrary", "arbitrary"), vmem_limit_bytes=VMEM_LIMIT),
        name="attn_in",
    )(*xs, mod, w["g_attn"], w["w_in"], w["gq_t"], w["gk_t"], w["gcq_t"], w["w_uq_t"], w["g_ckv"],
      w["w_uk"], w["w_uv_t"], tabs["ca"], tabs["sa"], tabs["cb"], tabs["sb"])


def _fold_rows(x, op):
    parts = [x[FOLD_ROWS * i:FOLD_ROWS * (i + 1)] for i in range(x.shape[0] // FOLD_ROWS)]
    while len(parts) > 1:
        parts = [op(parts[2 * i], parts[2 * i + 1]) for i in range(len(parts) // 2)]
    return parts[0]


def _key_slab(h):
    if isinstance(h, int):
        return 0 if h < N_HEADS_A else h - N_HEADS_A + 1
    return jnp.where(h < N_HEADS_A, 0, h - N_HEADS_A + 1)


def _value_row(h):
    per_kv = N_HEADS_A // N_KV_HEADS_A
    if isinstance(h, int):
        return HEAD_DV * (h // per_kv if h < N_HEADS_A else h - N_HEADS_A + N_KV_HEADS_A)
    return HEAD_DV * jnp.where(h < N_HEADS_A, h // per_kv, h - N_HEADS_A + N_KV_HEADS_A)


def _row_block(start, size):
    if isinstance(start, int):
        return pl.ds(start, size)
    return pl.ds(pl.multiple_of(start, size), size)


def _attention_kernel(q_ref, k_ref, v_ref, ot_ref, s0_ref, s1_ref):
    n_keys = k_ref.shape[2]
    w = s0_ref.shape[1]
    n_parts = ot_ref.shape[2] // w
    n_chunks = n_keys // KEY_CHUNK
    rows = lambda c: slice(KEY_CHUNK * c, KEY_CHUNK * (c + 1))
    cols = lambda t: slice(w * t, w * (t + 1))

    def stage(nxt, s_next_ref, cur, s_cur_ref, carry, fast):
        m_next = None
        if nxt is not None:
            h_next, t_next = nxt
            q_t = q_ref[0, _row_block(h_next * HEAD_SLOT, HEAD_SLOT), cols(t_next)]
            slab = _key_slab(h_next)
        if cur is not None:
            h_cur, t_cur = cur
            if not fast:
                m_b = jnp.broadcast_to(jnp.max(carry, axis=0, keepdims=True), (KEY_CHUNK, w))
            v_rows = _row_block(_value_row(h_cur), HEAD_DV)
            l_acc = jnp.zeros((FOLD_ROWS, w), F32)
            o_acc = jnp.zeros((HEAD_DV, w), F32)
        for c in range(n_chunks):
            if nxt is not None:
                s = jnp.dot(k_ref[0, slab, rows(c), :], q_t, preferred_element_type=F32)
                s_next_ref[rows(c), :] = s
                if not fast:
                    folded = _fold_rows(s, jnp.maximum)
                    m_next = folded if m_next is None else jnp.maximum(m_next, folded)
            if cur is not None:
                s_cur = s_cur_ref[rows(c), :]
                p = jnp.exp2(s_cur if fast else s_cur - m_b)
                l_acc = l_acc + _fold_rows(p, jnp.add)
                o_acc = o_acc + jnp.dot(v_ref[0, v_rows, rows(c)], p.astype(BF16),
                                        preferred_element_type=F32)
        if cur is not None:
            denom = jnp.sum(l_acc, axis=0, keepdims=True)
            ot_ref[0, _row_block(h_cur * HEAD_DV, HEAD_DV), cols(t_cur)] = (
                o_acc * (1.0 / denom)).astype(ot_ref.dtype)
            if fast:
                carry = (jnp.minimum(carry[0], denom), jnp.maximum(carry[1], denom))
        return carry if fast else m_next

    bufs = (s0_ref, s1_ref)

    def run(fast):
        def head(h, carry, last=False):
            for t in range(n_parts):
                nxt = (h, t + 1) if t + 1 < n_parts else (None if last else (h + 1, 0))
                carry = stage(nxt, bufs[(t + 1) % 2] if nxt is not None else None, (h, t), bufs[t % 2],
                              carry, fast)
            return carry

        init = (jnp.ones((1, w), F32), jnp.ones((1, w), F32)) if fast else None
        def heads(i, carry):
            for d in range(HEADS_PER_ITER):
                carry = head(HEADS_PER_ITER * i + d, carry)
            return carry

        n_iter = (N_HEADS - 1) // HEADS_PER_ITER
        carry = lax.fori_loop(0, n_iter, heads, stage((0, 0), s0_ref, None, None, init, fast))
        for h in range(n_iter * HEADS_PER_ITER, N_HEADS - 1):
            carry = head(h, carry)
        return head(N_HEADS - 1, carry, last=True)

    d_min, d_max = run(fast=True)
    in_range = jnp.logical_and(d_min >= 2.0 ** -SAFE_LOG2, d_max <= 2.0 ** SAFE_LOG2)
    redo = jnp.min(jnp.where(in_range, 1, 0)) == 0

    @pl.when(redo)
    def _():
        run(fast=False)


def _attention(q_all, k_all, v_all, tq):
    nb, _, s, _ = k_all.shape
    w = min(QUERY_ITEM, tq // 2)
    assert s % KEY_CHUNK == 0 and w % LANES == 0 and tq % (2 * w) == 0
    score_buf = pltpu.VMEM((s, w), F32)
    return pl.pallas_call(
        _attention_kernel,
        grid=(nb, s // tq),
        in_specs=[pl.BlockSpec((1, N_HEADS * HEAD_SLOT, tq), lambda b, i: (b, 0, i)),
                  pl.BlockSpec((1, N_KEY_SLABS, s, HEAD_SLOT), lambda b, i: (b, 0, 0, 0)),
                  pl.BlockSpec((1, V_ROWS, s), lambda b, i: (b, 0, 0))],
        out_specs=pl.BlockSpec((1, D_MIX, tq), lambda b, i: (b, 0, i)),
        out_shape=jax.ShapeDtypeStruct((nb, D_MIX, s), BF16),
        scratch_shapes=[score_buf, score_buf],
        compiler_params=pltpu.CompilerParams(
            dimension_semantics=("arbitrary", "arbitrary"), vmem_limit_bytes=VMEM_LIMIT),
        name="attention",
    )(q_all, k_all, v_all)


def _attn_out_kernel(ot_ref, *refs, n_x, n_prompt):
    x_refs = refs[:n_x]
    mod_ref, go_ref, wo_ref, gffn_ref, x1_ref, h2_ref = refs[n_x:]
    o_t = ot_ref[0].astype(F32)
    oa = o_t[:WIDTH_A]
    ob = o_t[WIDTH_A:]
    ra = _rsqrt_mean(jnp.sum(oa * oa, axis=0, keepdims=True), WIDTH_A)
    rb = _rsqrt_mean(jnp.sum(ob * ob, axis=0, keepdims=True), WIDTH_B)
    on = jnp.concatenate([oa * ra, ob * rb], axis=0).T * go_ref[0]
    res = jnp.dot(on.astype(BF16), wo_ref[0], preferred_element_type=F32)
    mod = mod_ref[0]
    gate_a = mod[:, 2 * D_MODEL:3 * D_MODEL]
    shift_f = mod[:, 3 * D_MODEL:4 * D_MODEL]
    scale_f = mod[:, 4 * D_MODEL:5 * D_MODEL]
    x1 = _load_tokens(x_refs, n_prompt) + gate_a * res
    x1_ref[0] = x1
    r = _rsqrt_mean(jnp.sum(x1 * x1, axis=-1, keepdims=True), D_MODEL)
    h2_ref[0] = ((x1 * r) * gffn_ref[0] * (1.0 + scale_f) + shift_f).astype(BF16)


def _attn_out(o_t, xs, mod, l, w, tm):
    nb, _, s = o_t.shape
    lay = lambda *blk: pl.BlockSpec((1,) + blk, lambda b, i: (l,) + (0,) * len(blk))
    tok = pl.BlockSpec((1, tm, D_MODEL), lambda b, i: (b, i, 0))
    return pl.pallas_call(
        functools.partial(_attn_out_kernel, n_x=len(xs), n_prompt=xs[0].shape[0]),
        grid=(nb, s // tm),
        in_specs=[pl.BlockSpec((1, D_MIX, tm), lambda b, i: (b, 0, i))] + _token_specs(xs, tm, s // tm) + [
            pl.BlockSpec((1, 1, N_MOD * D_MODEL), lambda b, i: (b, 0, 0)),
            lay(1, D_MIX),
            lay(D_MIX, D_MODEL),
            lay(1, D_MODEL),
        ],
        out_specs=(tok, tok),
        out_shape=(jax.ShapeDtypeStruct((nb, s, D_MODEL), F32), jax.ShapeDtypeStruct((nb, s, D_MODEL), BF16)),
        compiler_params=pltpu.CompilerParams(
            dimension_semantics=("arbitrary", "arbitrary"), vmem_limit_bytes=VMEM_LIMIT),
        name="attn_out",
    )(o_t, *xs, mod, w["g_out"], w["w_o"], w["g_ffn"])


def _ffn_kernel(hm_ref, hp_ref, hn_ref, x1_ref, mod_ref, wup_ref, cw_ref, cb_ref, wdn_ref, gfin_ref,
                out_ref, hext_ref, ua_ref, ub_ref, *, final):
    acc_ref = out_ref.at[0]
    i = pl.program_id(1)
    n_i = pl.num_programs(1)
    tm = hm_ref.shape[1]

    keep_prev = jnp.where(i > 0, 1.0, 0.0).astype(F32)
    keep_next = jnp.where(i < n_i - 1, 1.0, 0.0).astype(F32)
    hext_ref[0:HALO, :] = (hp_ref[0].astype(F32) * keep_prev).astype(BF16)
    hext_ref[HALO:HALO + tm, :] = hm_ref[0]
    hext_ref[HALO + tm:HALO + tm + HALO, :] = (hn_ref[0].astype(F32) * keep_next).astype(BF16)
    acc_ref[...] = jnp.zeros((tm, D_MODEL), F32)

    def up(c, u_ref):
        u = jnp.dot(hext_ref[...], wup_ref[0, c], preferred_element_type=F32)
        for j in range(FF_SLABS):
            u_ref[j] = u[:, LANES * j:LANES * (j + 1)]

    def mix(c, u_ref):
        cw = cw_ref[0, c]
        cb = cb_ref[0, c]
        ys = []
        for j in range(FF_SLABS):
            lanes = slice(LANES * j, LANES * (j + 1))
            ys.append(u_ref[j, HALO - 1:HALO - 1 + tm, :] * cw[0:1, lanes]
                      + u_ref[j, HALO:HALO + tm, :] * cw[1:2, lanes]
                      + u_ref[j, HALO + 1:HALO + 1 + tm, :] * cw[2:3, lanes] + cb[:, lanes])
        half = FF_SLABS // 2
        acts = []
        for j in range(half):
            gate = ys[half + j]
            acts.append((gate * (1.0 / (1.0 + jnp.exp(-gate)))) * ys[j])
        act = jnp.concatenate(acts, axis=1).astype(BF16)
        acc_ref[...] += jnp.dot(act, wdn_ref[0, c], preferred_element_type=F32)

    bufs = (ua_ref, ub_ref)
    up(0, ua_ref)
    for c in range(N_FF_CHUNKS):
        if c + 1 < N_FF_CHUNKS:
            up(c + 1, bufs[(c + 1) % 2])
        mix(c, bufs[c % 2])

    gate_f = mod_ref[0][:, 5 * D_MODEL:6 * D_MODEL]
    x2 = x1_ref[0] + gate_f * acc_ref[...]
    if final:
        r = _rsqrt_mean(jnp.sum(x2 * x2, axis=-1, keepdims=True), D_MODEL)
        x2 = (x2 * r) * gfin_ref[...]
    out_ref[0] = x2


def _ffn(h2, x1, mod, l, w, g_final, tm, final, b0=0, nb=None):
    nb = x1.shape[0] if nb is None else nb
    s = x1.shape[1]
    n_i = s // tm
    hb = tm // HALO
    tok = pl.BlockSpec((1, tm, D_MODEL), lambda b, i: (b + b0, i, 0))
    resident = lambda *blk: pl.BlockSpec((1,) + blk, lambda b, i: (l,) + (0,) * len(blk),
                                         pipeline_mode=pl.Buffered(1))
    u_buf = pltpu.VMEM((FF_SLABS, tm + 2 * HALO, LANES), F32)
    return pl.pallas_call(
        functools.partial(_ffn_kernel, final=final),
        grid=(nb, n_i),
        in_specs=[
            tok,
            pl.BlockSpec((1, HALO, D_MODEL), lambda b, i: (b + b0, jnp.maximum(i * hb - 1, 0), 0)),
            pl.BlockSpec((1, HALO, D_MODEL),
                         lambda b, i: (b + b0, jnp.minimum((i + 1) * hb, s // HALO - 1), 0)),
            tok,
            pl.BlockSpec((1, 1, N_MOD * D_MODEL), lambda b, i: (b + b0, 0, 0)),
            resident(N_FF_CHUNKS, D_MODEL, 2 * FF_CHUNK),
            resident(N_FF_CHUNKS, 3, 2 * FF_CHUNK),
            resident(N_FF_CHUNKS, 1, 2 * FF_CHUNK),
            resident(N_FF_CHUNKS, FF_CHUNK, D_MODEL),
            pl.BlockSpec((1, D_MODEL), lambda b, i: (0, 0)),
        ],
        out_specs=pl.BlockSpec((1, tm, D_MODEL), lambda b, i: (b, i, 0)),
        out_shape=jax.ShapeDtypeStruct((nb, s, D_MODEL), F32),
        scratch_shapes=[pltpu.VMEM((tm + 2 * HALO, D_MODEL), BF16), u_buf, u_buf],
        compiler_params=pltpu.CompilerParams(
            dimension_semantics=("arbitrary", "arbitrary"), vmem_limit_bytes=VMEM_LIMIT),
        name="ffn",
    )(h2, h2, h2, x1, mod, w["w_up"], w["conv_w"], w["conv_b"], w["w_down"], g_final)


def _rope_tables_t(n_tokens, rot_dim):
    rows = n_tokens // GRID_W
    row = jnp.repeat(jnp.arange(rows, dtype=F32), GRID_W)
    col = jnp.tile(jnp.arange(GRID_W, dtype=F32), rows)
    quarter = rot_dim // 4
    inv_freq = ROPE_THETA ** (-jnp.arange(quarter, dtype=F32) / quarter)
    ang = jnp.stack([row, col], axis=-1)[:, :, None] * inv_freq
    cos, sin = jnp.cos(ang), jnp.sin(ang)
    c = jnp.concatenate([cos[:, 0], cos[:, 0], cos[:, 1], cos[:, 1]], axis=-1).T
    s = jnp.concatenate([-sin[:, 0], sin[:, 0], -sin[:, 1], sin[:, 1]], axis=-1).T
    return c, s


def _prepare_weights(g_attn, w_in, g_q_a, g_k_a, g_cq, w_uq, g_ckv, w_ukv, g_out_a, g_out_b, w_o,
                     g_ffn, w_up, conv_w, conv_b, w_down, tm):
    o_kr = D_IN - QK_ROPE_DIM
    w_in_p = jnp.concatenate([
        w_in[..., :o_kr], jnp.zeros((DEPTH, D_MODEL, ROPE_SLOT_OFF), F32), w_in[..., o_kr:],
        jnp.zeros((DEPTH, D_MODEL, LANES - ROPE_SLOT_OFF - QK_ROPE_DIM), F32)], axis=-1).astype(BF16)
    hq = QK_NOPE_DIM + QK_ROPE_DIM
    w_uq_p = jnp.pad(w_uq.reshape(DEPTH, Q_LORA, N_HEADS_B, hq), ((0, 0), (0, 0), (0, 0), (0, HEAD_SLOT - hq)))
    w_uq_t = w_uq_p.reshape(DEPTH, Q_LORA, N_HEADS_B * HEAD_SLOT).transpose(0, 2, 1).astype(BF16)
    w_ukv_h = w_ukv.reshape(DEPTH, KV_LORA, N_HEADS_B, QK_NOPE_DIM + V_DIM_B)
    w_uk = jnp.pad(w_ukv_h[..., :QK_NOPE_DIM], ((0, 0), (0, 0), (0, 0), (0, HEAD_SLOT - QK_NOPE_DIM)))
    w_uk = w_uk.reshape(DEPTH, KV_LORA, N_HEADS_B * HEAD_SLOT).astype(BF16)
    w_uv_t = w_ukv_h[..., QK_NOPE_DIM:].reshape(DEPTH, KV_LORA, WIDTH_B).transpose(0, 2, 1).astype(BF16)
    def chunked(a):
        cols = lambda c, off: a[..., off + FF_CHUNK * c:off + FF_CHUNK * (c + 1)]
        return jnp.stack([jnp.concatenate([cols(c, 0), cols(c, D_FF)], axis=-1) for c in range(N_FF_CHUNKS)],
                         axis=1)

    w_up_c = chunked(w_up.astype(BF16))
    conv_w_c = chunked(conv_w)
    conv_b_c = chunked(conv_b[:, None, :])
    w_down_c = w_down.reshape(DEPTH, N_FF_CHUNKS, FF_CHUNK, D_MODEL).astype(BF16)
    col = lambda g: jnp.broadcast_to(g[:, :, None], g.shape + (tm,))
    return {
        "g_attn": g_attn.reshape(DEPTH, 1, D_MODEL), "w_in": w_in_p,
        "gq_t": col(g_q_a), "gk_t": col(g_k_a), "gcq_t": col(g_cq), "w_uq_t": w_uq_t,
        "g_ckv": g_ckv.reshape(DEPTH, 1, KV_LORA), "w_uk": w_uk, "w_uv_t": w_uv_t,
        "g_out": jnp.concatenate([g_out_a, g_out_b], axis=-1).reshape(DEPTH, 1, D_MIX),
        "w_o": w_o.astype(BF16), "g_ffn": g_ffn.reshape(DEPTH, 1, D_MODEL),
        "w_up": w_up_c, "conv_w": conv_w_c, "conv_b": conv_b_c, "w_down": w_down_c,
    }


def _tiles(s):
    return min(1024, s), min(2048, s), min(1024, s)


def kernel(x_prompt, x_sample, c_prompt, c_sample, w_ada, b_ada, g_attn, w_in, g_q_a, g_k_a, g_cq, w_uq,
           g_ckv, w_ukv, g_out_a, g_out_b, w_o, g_ffn, w_up, conv_w, conv_b, w_down, g_final):
    n_prompt = x_prompt.shape[0]
    s = x_prompt.shape[1]
    assert x_sample.shape[1] == s and s % GRID_W == 0
    tm, tq, tf = _tiles(s)
    assert all(s % t == 0 for t in (tm, tq, tf)) and tf % HALO == 0
    xs = (x_prompt, x_sample)
    c = jnp.concatenate([c_prompt, c_sample], axis=0)
    nb = c.shape[0]

    w = _prepare_weights(g_attn, w_in, g_q_a, g_k_a, g_cq, w_uq, g_ckv, w_ukv, g_out_a, g_out_b, w_o,
                         g_ffn, w_up, conv_w, conv_b, w_down, tm)
    ca, sa = _rope_tables_t(s, HEAD_DIM_A)
    cb, sb = _rope_tables_t(s, QK_ROPE_DIM)
    tabs = {"ca": ca, "sa": sa, "cb": cb, "sb": sb}
    mod_all = _modulation(c, w_ada, b_ada)
    g_fin = g_final.reshape(1, D_MODEL)

    for l in range(DEPTH):
        mod = mod_all[l].reshape(nb, 1, N_MOD * D_MODEL)
        q_all, k_all, v_all = _attn_in(xs, mod, l, w, tabs, tm)
        o_t = _attention(q_all, k_all, v_all, tq)
        x1, h2 = _attn_out(o_t, xs, mod, l, w, tm)
        if l < DEPTH - 1:
            xs = (_ffn(h2, x1, mod, l, w, g_fin, tf, final=False),)
    return (_ffn(h2, x1, mod, DEPTH - 1, w, g_fin, tf, final=True, b0=0, nb=n_prompt),
            _ffn(h2, x1, mod, DEPTH - 1, w, g_fin, tf, final=True, b0=n_prompt, nb=nb - n_prompt))
```
